```python
import jax, jax.numpy as jnp
from jax import lax
import numpy as np

D_MODEL = 2048
BATCH = 4
SEQ = 4096
DEPTH = 2

GRID_W = 64
CTX_LEN = 256
LRU_WIDTH = D_MODEL // 2
LRU_HEADS = 8
LRU_HEAD_DIM = LRU_WIDTH // LRU_HEADS
CONV_W = 4
CONV_LEFT = 2
RG_C = 8.0
MLP_WIDTH = D_MODEL - LRU_WIDTH
MLP_HEADS = 8
MLP_HEAD_DIM = MLP_WIDTH // MLP_HEADS
CHUNK = 128
ROWS_PER_CHUNK = CHUNK // GRID_W
MIX_WIDTH = LRU_WIDTH + MLP_WIDTH
IN_COLS = 2 * LRU_WIDTH + 2 * MLP_WIDTH
N_EXPERTS = 32
TOP_K = 4
D_FF = D_MODEL
SWIGLU_LIMIT = 7.0
SWIGLU_ALPHA = 1.702
EXPERT_BLOCK = 256
EPS = 1e-6

kernel_name = 'hybrid_rglru_chunkmlp_moe_prefix_dit'


def rms_norm(x, gain=None):
    xf = x.astype(jnp.float32)
    y = xf * lax.rsqrt(jnp.mean(xf * xf, axis=-1, keepdims=True) + EPS)
    if gain is not None:
        y = y * gain.astype(jnp.float32)
    return y.astype(x.dtype)


def depthwise_conv(x, w, b):
    n = x.shape[1]
    xp = jnp.pad(x, ((0, 0), (CONV_LEFT, CONV_W - 1 - CONV_LEFT), (0, 0)))
    y = b
    for k in range(CONV_W):
        y = y + xp[:, k:k + n] * w[k]
    return y


def _linear_combine(left, right):
    a1, b1 = left
    a2, b2 = right
    return a1 * a2, a2 * b1 + b2


def rg_lru(xs, w_r, b_r, w_i, b_i, lam, h0, reverse):
    bsz, n, ch = xs.shape
    xh = xs.reshape(bsz, n, LRU_HEADS, LRU_HEAD_DIM)
    r = jax.nn.sigmoid((jnp.einsum('blhi,hij->blhj', xh, w_r) + b_r).astype(jnp.float32)).reshape(bsz, n, ch)
    i = jax.nn.sigmoid((jnp.einsum('blhi,hij->blhj', xh, w_i) + b_i).astype(jnp.float32)).reshape(bsz, n, ch)
    log_a = -RG_C * r * jax.nn.softplus(-lam.astype(jnp.float32))
    a = jnp.exp(log_a)
    u = jnp.sqrt(-jnp.expm1(2.0 * log_a)) * i * xs.astype(jnp.float32)
    a_cum, h = lax.associative_scan(_linear_combine, (a, u), axis=1, reverse=reverse)
    return h + a_cum * h0[:, None, :]


def chunk_mix(v, w_s, b_s, n_chunks):
    bsz, n, _ = v.shape
    vh = v.reshape(bsz, n_chunks, CHUNK, MLP_HEADS, MLP_HEAD_DIM)
    s = jnp.einsum('hpq,bnqhc->bnphc', w_s, vh) + jnp.swapaxes(b_s, 0, 1)[:, :, None]
    return s.reshape(bsz, n, MLP_WIDTH)


def hybrid_mixer(nx, nc, w_in, conv_w, conv_b, w_r, b_r, w_i, b_i, lam, g_v, w_s, b_s,
                 g_lru, g_mlp, w_out, with_ctx_out):
    bsz, n_lat, _ = nx.shape
    n_ctx = nc.shape[1]
    R, M = LRU_WIDTH, MLP_WIDTH
    z = jnp.concatenate([nc, nx], axis=1) @ w_in
    zc, zx = z[:, :n_ctx], z[:, n_ctx:]
    rc = depthwise_conv(zc[..., :R], conv_w, conv_b)
    rx = depthwise_conv(zx[..., :R], conv_w, conv_b)
    zero = jnp.zeros((bsz, R), jnp.float32)
    hc_f = rg_lru(rc, w_r[0], b_r[0], w_i[0], b_i[0], lam[0], zero, False)
    hc_b = rg_lru(rc, w_r[1], b_r[1], w_i[1], b_i[1], lam[1], zero, True)
    hx_f = rg_lru(rx, w_r[0], b_r[0], w_i[0], b_i[0], lam[0], hc_f[:, -1], False)
    hx_b = rg_lru(rx, w_r[1], b_r[1], w_i[1], b_i[1], lam[1], hc_b[:, 0], True)

    def finish(zs, h_f, h_b, n_chunks):
        y_lru = ((h_f + h_b) * jax.nn.gelu(zs[..., R:2 * R].astype(jnp.float32))).astype(zs.dtype)
        u = jax.nn.gelu(zs[..., 2 * R:2 * R + M])
        v = rms_norm(jax.nn.gelu(zs[..., 2 * R + M:]), g_v)
        y_mlp = u * chunk_mix(v, w_s, b_s, n_chunks)
        y = jnp.concatenate([rms_norm(y_lru, g_lru), rms_norm(y_mlp, g_mlp)], axis=-1)
        return y @ w_out

    rows = n_lat // GRID_W
    y_lat = finish(zx, hx_f, hx_b, rows // ROWS_PER_CHUNK)
    y_ctx = finish(zc, hc_f, hc_b, n_ctx // CHUNK) if with_ctx_out else None
    return y_lat, y_ctx


def moe_ffn(xt, w_router, b_router, w_gu, b_gu, w_down, b_down):
    n, d = xt.shape
    logits = (xt @ w_router + b_router).astype(jnp.float32)
    top_logits, top_idx = lax.top_k(logits, TOP_K)
    top_w = jax.nn.softmax(top_logits, axis=-1)
    n_assign = n * TOP_K
    flat_e = top_idx.reshape(-1)
    flat_tok = jnp.repeat(jnp.arange(n, dtype=jnp.int32), TOP_K)
    flat_w = top_w.reshape(-1)
    order = jnp.argsort(flat_e)
    sorted_e = flat_e[order]
    counts = jnp.bincount(flat_e, length=N_EXPERTS)
    padded = (counts + EXPERT_BLOCK - 1) // EXPERT_BLOCK * EXPERT_BLOCK
    ends_p = jnp.cumsum(padded)
    starts_p = ends_p - padded
    starts = jnp.cumsum(counts) - counts
    dest = starts_p[sorted_e] + jnp.arange(n_assign, dtype=jnp.int32) - starts[sorted_e]
    n_blocks = (n_assign + N_EXPERTS * (EXPERT_BLOCK - 1) + EXPERT_BLOCK - 1) // EXPERT_BLOCK
    n_rows = n_blocks * EXPERT_BLOCK
    row_tok = jnp.zeros((n_rows,), jnp.int32).at[dest].set(flat_tok[order])
    row_w = jnp.zeros((n_rows,), jnp.float32).at[dest].set(flat_w[order])
    block_start = jnp.arange(n_blocks, dtype=ends_p.dtype) * EXPERT_BLOCK
    block_e = jnp.minimum(jnp.searchsorted(ends_p, block_start, side='right'), N_EXPERTS - 1)

    def expert_block(args):
        tok, wt, e = args
        xb = xt[tok]
        gu = (xb @ w_gu[e] + b_gu[e]).astype(jnp.float32)
        gate = jnp.minimum(gu[:, :D_FF], SWIGLU_LIMIT)
        up = jnp.clip(gu[:, D_FF:], -SWIGLU_LIMIT, SWIGLU_LIMIT)
        act = (up + 1.0) * gate * jax.nn.sigmoid(SWIGLU_ALPHA * gate)
        out = (act.astype(xt.dtype) @ w_down[e] + b_down[e]).astype(jnp.float32)
        return out * wt[:, None]

    outs = lax.map(expert_block, (row_tok.reshape(n_blocks, EXPERT_BLOCK),
                                  row_w.reshape(n_blocks, EXPERT_BLOCK), block_e))
    y = jax.ops.segment_sum(outs.reshape(n_rows, d), row_tok, num_segments=n)
    return y.astype(xt.dtype)


def setup_inputs(seed: int = 0) -> dict:
    key = jax.random.key(seed)
    ks = jax.random.split(key, 32)
    f32 = jnp.float32

    def nrm(k, shape, scale):
        return jax.random.normal(k, shape, f32) * scale

    R, M, D, E = LRU_WIDTH, MLP_WIDTH, D_MODEL, N_EXPERTS
    s = jax.random.uniform(ks[8], (DEPTH, 2, R), f32, minval=0.9, maxval=0.999) ** (1.0 / RG_C)
    return {
        'x': nrm(ks[0], (BATCH, SEQ, D), 1.0),
        'c': nrm(ks[1], (BATCH, D), 1.0),
        'ctx': nrm(ks[2], (BATCH, CTX_LEN, D), 1.0),
        'c_ctx': nrm(ks[3], (D,), 1.0),
        'w_mod': nrm(ks[4], (DEPTH, D, 6 * D), 0.5 * D ** -0.5),
        'b_mod': nrm(ks[5], (DEPTH, 6 * D), 0.02),
        'w_in': nrm(ks[6], (DEPTH, D, IN_COLS), D ** -0.5),
        'conv_w': nrm(ks[7], (DEPTH, CONV_W, R), CONV_W ** -0.5),
        'conv_b': nrm(ks[9], (DEPTH, R), 0.02),
        'w_r': nrm(ks[10], (DEPTH, 2, LRU_HEADS, LRU_HEAD_DIM, LRU_HEAD_DIM), LRU_HEAD_DIM ** -0.5),
        'b_r': nrm(ks[11], (DEPTH, 2, LRU_HEADS, LRU_HEAD_DIM), 0.02),
        'w_i': nrm(ks[12], (DEPTH, 2, LRU_HEADS, LRU_HEAD_DIM, LRU_HEAD_DIM), LRU_HEAD_DIM ** -0.5),
        'b_i': nrm(ks[13], (DEPTH, 2, LRU_HEADS, LRU_HEAD_DIM), 0.02),
        'lam': jnp.log(s) - jnp.log1p(-s),
        'g_v': 1.0 + nrm(ks[14], (DEPTH, M), 0.02),
        'w_s': nrm(ks[15], (DEPTH, MLP_HEADS, CHUNK, CHUNK), CHUNK ** -0.5),
        'b_s': 1.0 + nrm(ks[16], (DEPTH, MLP_HEADS, CHUNK), 0.02),
        'g_lru': 1.0 + nrm(ks[17], (DEPTH, R), 0.02),
        'g_mlp': 1.0 + nrm(ks[18], (DEPTH, M), 0.02),
        'w_out': nrm(ks[19], (DEPTH, MIX_WIDTH, D), MIX_WIDTH ** -0.5),
        'w_router': nrm(ks[20], (DEPTH, D, E), D ** -0.5),
        'b_router': nrm(ks[21], (DEPTH, E), 0.01),
        'w_gu': nrm(ks[22], (DEPTH, E, D, 2 * D_FF), D ** -0.5),
        'b_gu': nrm(ks[23], (DEPTH, E, 2 * D_FF), 0.02),
        'w_down': nrm(ks[24], (DEPTH, E, D_FF, D), D_FF ** -0.5),
        'b_down': nrm(ks[25], (DEPTH, E, D), 0.02),
        'g_final': 1.0 + nrm(ks[26], (D,), 0.02),
    }


def reference(x, c, ctx, c_ctx, w_mod, b_mod, w_in, conv_w, conv_b, w_r, b_r, w_i, b_i, lam,
              g_v, w_s, b_s, g_lru, g_mlp, w_out, w_router, b_router, w_gu, b_gu, w_down, b_down,
              g_final):
    bsz, n_lat, d = x.shape
    n_ctx = ctx.shape[1]
    h_ctx = ctx
    s_lat = jax.nn.silu(c)
    s_ctx = jax.nn.silu(c_ctx)
    for l in range(DEPTH):
        last = l == DEPTH - 1
        sh1x, sc1x, g1x, sh2x, sc2x, g2x = [m[:, None, :] for m in jnp.split(s_lat @ w_mod[l] + b_mod[l], 6, axis=-1)]
        sh1c, sc1c, g1c, sh2c, sc2c, g2c = jnp.split(s_ctx @ w_mod[l] + b_mod[l], 6, axis=-1)
        nx = rms_norm(x) * (1.0 + sc1x) + sh1x
        nc = rms_norm(h_ctx) * (1.0 + sc1c) + sh1c
        y_lat, y_ctx = hybrid_mixer(nx, nc, w_in[l], conv_w[l], conv_b[l], w_r[l], b_r[l], w_i[l], b_i[l],
                                    lam[l], g_v[l], w_s[l], b_s[l], g_lru[l], g_mlp[l], w_out[l],
                                    not last)
        x = x + g1x * y_lat
        nx2 = rms_norm(x) * (1.0 + sc2x) + sh2x
        if last:
            y = moe_ffn(nx2.reshape(-1, d), w_router[l], b_router[l], w_gu[l], b_gu[l], w_down[l], b_down[l])
            x = x + g2x * y.reshape(bsz, n_lat, d)
        else:
            h_ctx = h_ctx + g1c * y_ctx
            nc2 = rms_norm(h_ctx) * (1.0 + sc2c) + sh2c
            tokens = jnp.concatenate([nc2, nx2], axis=1).reshape(-1, d)
            y = moe_ffn(tokens, w_router[l], b_router[l], w_gu[l], b_gu[l], w_down[l], b_down[l])
            y = y.reshape(bsz, n_ctx + n_lat, d)
            h_ctx = h_ctx + g2c * y[:, :n_ctx]
            x = x + g2x * y[:, n_ctx:]
    return rms_norm(x, g_final)
```

```python
import functools
import math

import jax
import jax.numpy as jnp
from jax import lax
from jax.experimental import pallas as pl
from jax.experimental.pallas import tpu as pltpu

LRU_HEADS = 8
MLP_HEADS = 8
CONV_W = 4
CONV_LEFT = 2
RG_C = 8.0
CHUNK = 128
TOP_K = 4
SWIGLU_LIMIT = 7.0
SWIGLU_ALPHA = 1.702
EPS = 1e-6

SUBLANES = 8
LANES = 128
TIME_TILE = 256
EXPERT_TILE = 256
COMBINE_TILE = 128
VMEM_LIMIT = 56 * 1024 * 1024

F32 = jnp.float32
BF16 = jnp.bfloat16


def _rms(x):
    return x * lax.rsqrt(jnp.mean(x * x, axis=-1, keepdims=True) + EPS)


def _gelu(x):
    c = math.sqrt(2.0 / math.pi)
    return 0.5 * x * (1.0 + jnp.tanh(c * (x + 0.044715 * (x * x * x))))


def _softplus(x):
    return jnp.maximum(x, 0.0) + jnp.log1p(jnp.exp(-jnp.abs(x)))


def _mod_kernel(c_ref, w_ref, b_ref, o_ref):
    c = c_ref[...]
    s = c * jax.nn.sigmoid(c)
    o_ref[0] = jnp.dot(s, w_ref[0], preferred_element_type=F32) + b_ref[0]


def _modulation(c_all, w_mod, b_mod):
    depth, d, n6 = w_mod.shape
    rows = c_all.shape[0]
    tn = n6 // 8
    return pl.pallas_call(
        _mod_kernel,
        grid=(depth, n6 // tn),
        in_specs=[
            pl.BlockSpec((rows, d), lambda l, n: (0, 0)),
            pl.BlockSpec((1, d, tn), lambda l, n: (l, 0, n)),
            pl.BlockSpec((1, 1, tn), lambda l, n: (l, 0, n)),
        ],
        out_specs=pl.BlockSpec((1, rows, tn), lambda l, n: (l, 0, n)),
        out_shape=jax.ShapeDtypeStruct((depth, rows, n6), F32),
        compiler_params=pltpu.CompilerParams(
            dimension_semantics=("arbitrary", "arbitrary"), vmem_limit_bytes=VMEM_LIMIT),
        name="modulation",
    )(c_all, w_mod, b_mod.reshape(depth, 1, n6))


def _inproj_kernel(x_ref, mod_ref, w_ref, z_ref, *, n_chunks):
    x = x_ref[0]
    m = mod_ref[0, 0]
    nx = _rms(x) * (1.0 + m[1:2]) + m[0:1]
    nb = nx.astype(BF16)
    cw = w_ref.shape[1] // n_chunks
    for c in range(n_chunks):
        z_ref[0, :, c * cw:(c + 1) * cw] = jnp.dot(
            nb, w_ref[:, c * cw:(c + 1) * cw], preferred_element_type=F32)


def _inproj(x, modt, w_in_bf16, n_ctx_tiles):
    bsz, s, d = x.shape
    n_cols = w_in_bf16.shape[1]
    tt = TIME_TILE
    return pl.pallas_call(
        functools.partial(_inproj_kernel, n_chunks=4),
        grid=(bsz, s // tt),
        in_specs=[
            pl.BlockSpec((1, tt, d), lambda b, i: (b, i, 0)),
            pl.BlockSpec((1, 1, 6, d), lambda b, i: (b, jnp.where(i < n_ctx_tiles, 0, 1), 0, 0)),
            pl.BlockSpec((d, n_cols), lambda b, i: (0, 0), pipeline_mode=pl.Buffered(1)),
        ],
        out_specs=pl.BlockSpec((1, tt, n_cols), lambda b, i: (b, i, 0)),
        out_shape=jax.ShapeDtypeStruct((bsz, s, n_cols), F32),
        compiler_params=pltpu.CompilerParams(
            dimension_semantics=("parallel", "arbitrary"), vmem_limit_bytes=VMEM_LIMIT),
        name="inproj",
    )(x, modt, w_in_bf16)


def _conv_tile(main, prev8, next8, cw, cb):
    tt = main.shape[0]
    row8 = lax.broadcasted_iota(jnp.int32, (SUBLANES, main.shape[1]), 0)
    w0, w1, w2, w3 = cw[0:1], cw[1:2], cw[2:3], cw[3:4]
    r1 = pltpu.roll(main, 1, 0)
    r2 = pltpu.roll(main, 2, 0)
    rm1 = pltpu.roll(main, tt - 1, 0)
    body = cb + w0 * r2 + w1 * r1 + w2 * main + w3 * rm1
    head = main[0:SUBLANES]
    h1 = jnp.where(row8 < 1, pltpu.roll(prev8, 1, 0), pltpu.roll(head, 1, 0))
    h2 = jnp.where(row8 < 2, pltpu.roll(prev8, 2, 0), pltpu.roll(head, 2, 0))
    head_out = cb + w0 * h2 + w1 * h1 + w2 * head + w3 * rm1[0:SUBLANES]
    tail_m1 = jnp.where(row8 == SUBLANES - 1, pltpu.roll(next8, SUBLANES - 1, 0), rm1[tt - SUBLANES:tt])
    tail_out = (cb + w0 * r2[tt - SUBLANES:tt] + w1 * r1[tt - SUBLANES:tt]
                + w2 * main[tt - SUBLANES:tt] + w3 * tail_m1)
    return body, head_out, tail_out


def _gates_tile(rc_ref, a_ref, u_ref, wg_ref, bg_ref, sp, direction):
    hd = rc_ref.shape[1] // LRU_HEADS
    for h in range(LRU_HEADS):
        cols = slice(h * hd, (h + 1) * hd)
        xh = rc_ref[:, cols]
        pre = jnp.dot(xh.astype(BF16), wg_ref[direction, h], preferred_element_type=F32) + bg_ref[direction, h]
        r = jax.nn.sigmoid(pre[:, :hd])
        gi = jax.nn.sigmoid(pre[:, hd:])
        a = jnp.exp((-RG_C) * r * sp[:, cols])
        a_ref[:, cols] = a
        u_ref[:, cols] = jnp.sqrt(1.0 - a * a) * gi * xh


def _scan_tile(a_ref, u_ref, out_ref, carry_ref, reverse):
    tt, width = a_ref.shape
    groups = tt // SUBLANES
    row8 = lax.broadcasted_iota(jnp.int32, (SUBLANES, width), 0)

    def body(g, carry):
        gi = (groups - 1 - g) if reverse else g
        sl = pl.ds(pl.multiple_of(gi * SUBLANES, SUBLANES), SUBLANES)
        a = a_ref[sl, :]
        u = u_ref[sl, :]
        for k in (1, 2, 4):
            if reverse:
                shift = SUBLANES - k
                valid = row8 < SUBLANES - k
            else:
                shift = k
                valid = row8 >= k
            a_s = pltpu.roll(a, shift, 0)
            u_s = pltpu.roll(u, shift, 0)
            u = jnp.where(valid, a * u_s + u, u)
            a = jnp.where(valid, a * a_s, a)
        h = a * carry + u
        out_ref[0, sl, :] = h
        return h[0:1] if reverse else h[SUBLANES - 1:SUBLANES]

    carry_ref[...] = lax.fori_loop(0, groups, body, carry_ref[...])


def _scan_kernel(zf_ref, zfp_ref, zfn_ref, zb_ref, zbp_ref, zbn_ref, cw_ref, cb_ref, wg_ref, bg_ref,
                 lam_ref, hf_ref, hb_ref, rc_ref, a_ref, u_ref, cf_ref, cbk_ref, *, n_ctx_tiles, n_tiles):
    i = pl.program_id(1)
    tt = rc_ref.shape[0]

    @pl.when(i == 0)
    def _():
        cf_ref[...] = jnp.zeros_like(cf_ref)
        cbk_ref[...] = jnp.zeros_like(cbk_ref)

    jb = jnp.where(i < n_ctx_tiles, n_ctx_tiles - 1 - i, n_tiles - 1 - (i - n_ctx_tiles))
    cw = cw_ref[...]
    cb = cb_ref[...]
    sp = _softplus(-lam_ref[...])

    def run(z_ref, zp_ref, zn_ref, j, direction, out_ref, carry_ref):
        first = jnp.logical_or(j == 0, j == n_ctx_tiles)
        last = jnp.logical_or(j == n_ctx_tiles - 1, j == n_tiles - 1)
        prev8 = zp_ref[0] * jnp.where(first, 0.0, 1.0)
        next8 = zn_ref[0] * jnp.where(last, 0.0, 1.0)
        body, head_out, tail_out = _conv_tile(z_ref[0], prev8, next8, cw, cb)
        rc_ref[...] = body
        rc_ref[0:SUBLANES] = head_out
        rc_ref[tt - SUBLANES:tt] = tail_out
        _gates_tile(rc_ref, a_ref, u_ref, wg_ref, bg_ref, sp[direction:direction + 1], direction)
        _scan_tile(a_ref, u_ref, out_ref, carry_ref, reverse=(direction == 1))

    run(zf_ref, zfp_ref, zfn_ref, i, 0, hf_ref, cf_ref)
    run(zb_ref, zbp_ref, zbn_ref, jb, 1, hb_ref, cbk_ref)


def _scan(z, conv_w, conv_b, wg, bg, lam, n_ctx_tiles):
    bsz, s, _ = z.shape
    r = conv_w.shape[1]
    tt = TIME_TILE
    n_tiles = s // tt
    per = tt // SUBLANES
    last8 = s // SUBLANES - 1

    def bwd_tile(i):
        return jnp.where(i < n_ctx_tiles, n_ctx_tiles - 1 - i, n_tiles - 1 - (i - n_ctx_tiles))

    main_f = pl.BlockSpec((1, tt, r), lambda b, i: (b, i, 0))
    prev_f = pl.BlockSpec((1, SUBLANES, r), lambda b, i: (b, jnp.maximum(i * per - 1, 0), 0))
    next_f = pl.BlockSpec((1, SUBLANES, r), lambda b, i: (b, jnp.minimum((i + 1) * per, last8), 0))
    main_b = pl.BlockSpec((1, tt, r), lambda b, i: (b, bwd_tile(i), 0))
    prev_b = pl.BlockSpec((1, SUBLANES, r), lambda b, i: (b, jnp.maximum(bwd_tile(i) * per - 1, 0), 0))
    next_b = pl.BlockSpec((1, SUBLANES, r), lambda b, i: (b, jnp.minimum((bwd_tile(i) + 1) * per, last8), 0))
    full = lambda shape: pl.BlockSpec(shape, lambda b, i: (0,) * len(shape))
    return pl.pallas_call(
        functools.partial(_scan_kernel, n_ctx_tiles=n_ctx_tiles, n_tiles=n_tiles),
        grid=(bsz, n_tiles),
        in_specs=[main_f, prev_f, next_f, main_b, prev_b, next_b,
                  full(conv_w.shape), full(conv_b.shape), full(wg.shape), full(bg.shape), full(lam.shape)],
        out_specs=[pl.BlockSpec((1, tt, r), lambda b, i: (b, i, 0)),
                   pl.BlockSpec((1, tt, r), lambda b, i: (b, bwd_tile(i), 0))],
        out_shape=[jax.ShapeDtypeStruct((bsz, s, r), F32), jax.ShapeDtypeStruct((bsz, s, r), F32)],
        scratch_shapes=[pltpu.VMEM((tt, r), F32), pltpu.VMEM((tt, r), F32), pltpu.VMEM((tt, r), F32),
                        pltpu.VMEM((1, r), F32), pltpu.VMEM((1, r), F32)],
        compiler_params=pltpu.CompilerParams(
            dimension_semantics=("parallel", "arbitrary"), vmem_limit_bytes=VMEM_LIMIT),
        name="lru_scan",
    )(z, z, z, z, z, z, conv_w, conv_b, wg, bg, lam)


def _finish_kernel(zg_ref, zu_ref, zv_ref, hf_ref, hb_ref, x_ref, mod_ref, glru_ref, gmlp_ref, gv_ref,
                   ws_ref, bs_ref, wout_ref, wr_ref, br_ref,
                   xo_ref, tok_ref, idx_ref, wt_ref, ym_ref, *, n_experts):
    tt, r = ym_ref.shape
    hd = r // MLP_HEADS
    y_lru = (hf_ref[0] + hb_ref[0]) * _gelu(zg_ref[0])
    yl = _rms(y_lru) * glru_ref[...]
    u = _gelu(zu_ref[0])
    vb = (_rms(_gelu(zv_ref[0])) * gv_ref[...]).astype(BF16)
    for ch in range(tt // CHUNK):
        rows = slice(ch * CHUNK, (ch + 1) * CHUNK)
        for h in range(MLP_HEADS):
            cols = slice(h * hd, (h + 1) * hd)
            mixed = jnp.dot(ws_ref[h], vb[rows, cols], preferred_element_type=F32) + bs_ref[h]
            ym_ref[rows, cols] = u[rows, cols] * mixed
    ym = _rms(ym_ref[...]) * gmlp_ref[...]
    y = (jnp.dot(yl.astype(BF16), wout_ref[0:r], preferred_element_type=F32)
         + jnp.dot(ym.astype(BF16), wout_ref[r:2 * r], preferred_element_type=F32))
    m = mod_ref[0, 0]
    xnew = x_ref[0] + m[2:3] * y
    xo_ref[0] = xnew
    tok = _rms(xnew) * (1.0 + m[4:5]) + m[3:4]
    tok_ref[0] = tok

    t_hi = tok.astype(BF16)
    t_lo = (tok - t_hi.astype(F32)).astype(BF16)
    logits = (jnp.dot(t_hi, wr_ref[0], preferred_element_type=F32)
              + jnp.dot(t_lo, wr_ref[0], preferred_element_type=F32)
              + jnp.dot(t_hi, wr_ref[1], preferred_element_type=F32)) + br_ref[...]
    lane = lax.broadcasted_iota(jnp.int32, logits.shape, 1)
    neg = jnp.float32(-jnp.inf)
    work = jnp.where(lane < n_experts, logits, neg)
    vals, idxs = [], []
    for _ in range(TOP_K):
        mx = jnp.max(work, axis=-1, keepdims=True)
        ix = jnp.min(jnp.where(work == mx, lane, LANES), axis=-1, keepdims=True)
        vals.append(mx)
        idxs.append(ix)
        work = jnp.where(lane == ix, neg, work)
    exps = [jnp.exp(v - vals[0]) for v in vals]
    denom = exps[0] + exps[1] + exps[2] + exps[3]
    idx_out = jnp.zeros(logits.shape, jnp.int32)
    wt_out = jnp.zeros(logits.shape, F32)
    for k in range(TOP_K):
        idx_out = jnp.where(lane == k, idxs[k], idx_out)
        wt_out = jnp.where(lane == k, exps[k] / denom, wt_out)
    idx_ref[0] = idx_out
    wt_ref[0] = wt_out


def _finish(z, hf, hb, x, modt, g_lru, g_mlp, g_v, ws, bs, w_out_bf16, wr_split, br_pad, n_experts,
            tile_offset, n_ctx_tiles):
    bsz, s, d = x.shape
    r = hf.shape[2]
    tt = TIME_TILE
    n_out_tiles = s // tt - tile_offset
    rows_out = n_out_tiles * tt
    off = tile_offset

    def zcol(c):
        return pl.BlockSpec((1, tt, r), lambda b, i: (b, i + off, c))

    full = lambda shape: pl.BlockSpec(shape, lambda b, i: (0,) * len(shape))
    tile_in = lambda w: pl.BlockSpec((1, tt, w), lambda b, i: (b, i + off, 0))
    tile_out = lambda w: pl.BlockSpec((1, tt, w), lambda b, i: (b, i, 0))
    return pl.pallas_call(
        functools.partial(_finish_kernel, n_experts=n_experts),
        grid=(bsz, n_out_tiles),
        in_specs=[zcol(1), zcol(2), zcol(3), tile_in(r), tile_in(r), tile_in(d),
                  pl.BlockSpec((1, 1, 6, d), lambda b, i: (b, jnp.where(i + off < n_ctx_tiles, 0, 1), 0, 0)),
                  full(g_lru.shape), full(g_mlp.shape), full(g_v.shape), full(ws.shape), full(bs.shape),
                  pl.BlockSpec(w_out_bf16.shape, lambda b, i: (0, 0), pipeline_mode=pl.Buffered(1)),
                  full(wr_split.shape), full(br_pad.shape)],
        out_specs=[tile_out(d), tile_out(d), tile_out(LANES), tile_out(LANES)],
        out_shape=[jax.ShapeDtypeStruct((bsz, rows_out, d), F32),
                   jax.ShapeDtypeStruct((bsz, rows_out, d), F32),
                   jax.ShapeDtypeStruct((bsz, rows_out, LANES), jnp.int32),
                   jax.ShapeDtypeStruct((bsz, rows_out, LANES), F32)],
        scratch_shapes=[pltpu.VMEM((tt, r), F32)],
        compiler_params=pltpu.CompilerParams(
            dimension_semantics=("parallel", "arbitrary"), vmem_limit_bytes=VMEM_LIMIT),
        name="mixer_finish",
    )(z, z, z, hf, hb, x, modt, g_lru, g_mlp, g_v, ws, bs, w_out_bf16, wr_split, br_pad)


def _gather_rows_kernel(nused_ref, idx_ref, tok_ref, out_ref, sem):
    j = pl.program_id(0)
    tm = out_ref.shape[0]

    @pl.when(j < nused_ref[0])
    def _():
        def issue(r, _):
            t = idx_ref[0, 0, r]
            pltpu.make_async_copy(tok_ref.at[pl.ds(t, 1)], out_ref.at[pl.ds(r, 1)], sem).start()
            return 0

        lax.fori_loop(0, tm, issue, 0)

        def drain(r, _):
            pltpu.make_async_copy(tok_ref.at[pl.ds(0, 1)], out_ref.at[pl.ds(r, 1)], sem).wait()
            return 0

        lax.fori_loop(0, tm, drain, 0)

    @pl.when(j >= nused_ref[0])
    def _():
        out_ref[...] = jnp.zeros_like(out_ref)


def _gather_rows(tok, row_tok, n_used):
    n_rows = row_tok.shape[0]
    d = tok.shape[1]
    tm = EXPERT_TILE
    n_blocks = n_rows // tm
    return pl.pallas_call(
        _gather_rows_kernel,
        grid_spec=pltpu.PrefetchScalarGridSpec(
            num_scalar_prefetch=1,
            grid=(n_blocks,),
            in_specs=[pl.BlockSpec((1, 1, tm), lambda j, nu: (j, 0, 0), memory_space=pltpu.SMEM),
                      pl.BlockSpec(memory_space=pl.ANY)],
            out_specs=pl.BlockSpec((tm, d), lambda j, nu: (j, 0)),
            scratch_shapes=[pltpu.SemaphoreType.DMA(())],
        ),
        out_shape=jax.ShapeDtypeStruct((n_rows, d), tok.dtype),
        compiler_params=pltpu.CompilerParams(dimension_semantics=("arbitrary",)),
        name="expert_gather",
    )(n_used, row_tok.reshape(n_blocks, 1, tm), tok)


def _expert_up_kernel(be_ref, nused_ref, x_ref, wg_ref, wu_ref, bgate_ref, bup_ref, h_ref, wg_s, wu_s):
    j = pl.program_id(1)
    changed = jnp.logical_or(j == 0, be_ref[j] != be_ref[jnp.maximum(j - 1, 0)])

    @pl.when(changed)
    def _():
        wg_s[...] = wg_ref[0].astype(BF16)
        wu_s[...] = wu_ref[0].astype(BF16)

    @pl.when(j < nused_ref[0])
    def _():
        xb = x_ref[...].astype(BF16)
        gate = jnp.dot(xb, wg_s[...], preferred_element_type=F32) + bgate_ref[0]
        up = jnp.dot(xb, wu_s[...], preferred_element_type=F32) + bup_ref[0]
        gate = jnp.minimum(gate, SWIGLU_LIMIT)
        up = jnp.clip(up, -SWIGLU_LIMIT, SWIGLU_LIMIT)
        act = (up + 1.0) * gate * jax.nn.sigmoid(SWIGLU_ALPHA * gate)
        h_ref[...] = act.astype(BF16)

    @pl.when(j >= nused_ref[0])
    def _():
        h_ref[...] = jnp.zeros_like(h_ref)


def _expert_up(xs, w_gu, b_gu, block_e, n_used):
    n_rows, d = xs.shape
    n_exp, _, two_f = w_gu.shape
    f = two_f // 2
    tm = EXPERT_TILE
    tn = 512
    n_blocks = n_rows // tm
    nf = f // tn
    return pl.pallas_call(
        _expert_up_kernel,
        grid_spec=pltpu.PrefetchScalarGridSpec(
            num_scalar_prefetch=2,
            grid=(nf, n_blocks),
            in_specs=[pl.BlockSpec((tm, d), lambda n, j, be, nu: (j, 0)),
                      pl.BlockSpec((1, d, tn), lambda n, j, be, nu: (be[j], 0, n)),
                      pl.BlockSpec((1, d, tn), lambda n, j, be, nu: (be[j], 0, nf + n)),
                      pl.BlockSpec((1, 1, tn), lambda n, j, be, nu: (be[j], 0, n)),
                      pl.BlockSpec((1, 1, tn), lambda n, j, be, nu: (be[j], 0, nf + n))],
            out_specs=pl.BlockSpec((tm, tn), lambda n, j, be, nu: (j, n)),
            scratch_shapes=[pltpu.VMEM((d, tn), BF16), pltpu.VMEM((d, tn), BF16)],
        ),
        out_shape=jax.ShapeDtypeStruct((n_rows, f), BF16),
        compiler_params=pltpu.CompilerParams(
            dimension_semantics=("arbitrary", "arbitrary"), vmem_limit_bytes=VMEM_LIMIT),
        name="expert_up",
    )(block_e, n_used, xs, w_gu, w_gu, b_gu.reshape(n_exp, 1, two_f), b_gu.reshape(n_exp, 1, two_f))


def _expert_down_kernel(be_ref, nused_ref, h_ref, wd_ref, bd_ref, rw_ref, o_ref, wd_s):
    j = pl.program_id(1)
    changed = jnp.logical_or(j == 0, be_ref[j] != be_ref[jnp.maximum(j - 1, 0)])

    @pl.when(changed)
    def _():
        wd_s[...] = wd_ref[0].astype(BF16)

    @pl.when(j < nused_ref[0])
    def _():
        out = jnp.dot(h_ref[...], wd_s[...], preferred_element_type=F32) + bd_ref[0]
        o_ref[...] = out * rw_ref[...]

    @pl.when(j >= nused_ref[0])
    def _():
        o_ref[...] = jnp.zeros_like(o_ref)


def _expert_down(h, w_down, b_down, row_w, block_e, n_used):
    n_rows, f = h.shape
    n_exp, _, d = w_down.shape
    tm = EXPERT_TILE
    tn = 1024
    n_blocks = n_rows // tm
    return pl.pallas_call(
        _expert_down_kernel,
        grid_spec=pltpu.PrefetchScalarGridSpec(
            num_scalar_prefetch=2,
            grid=(d // tn, n_blocks),
            in_specs=[pl.BlockSpec((tm, f), lambda n, j, be, nu: (j, 0)),
                      pl.BlockSpec((1, f, tn), lambda n, j, be, nu: (be[j], 0, n)),
                      pl.BlockSpec((1, 1, tn), lambda n, j, be, nu: (be[j], 0, n)),
                      pl.BlockSpec((tm, 1), lambda n, j, be, nu: (j, 0))],
            out_specs=pl.BlockSpec((tm, tn), lambda n, j, be, nu: (j, n)),
            scratch_shapes=[pltpu.VMEM((f, tn), BF16)],
        ),
        out_shape=jax.ShapeDtypeStruct((n_rows, d), F32),
        compiler_params=pltpu.CompilerParams(
            dimension_semantics=("arbitrary", "arbitrary"), vmem_limit_bytes=VMEM_LIMIT),
        name="expert_down",
    )(block_e, n_used, h, w_down, b_down.reshape(n_exp, 1, d), row_w.reshape(n_rows, 1))


def _combine_kernel(pos_ref, outs_ref, x_ref, mod_ref, gfin_ref, o_ref, buf, sem, *, final):
    tb = x_ref.shape[0]

    def issue(r, _):
        for k in range(TOP_K):
            p = pos_ref[0, 0, r * TOP_K + k]
            pltpu.make_async_copy(outs_ref.at[pl.ds(p, 1)], buf.at[k, pl.ds(r, 1)], sem).start()
        return 0

    lax.fori_loop(0, tb, issue, 0)

    def drain(r, _):
        for k in range(TOP_K):
            pltpu.make_async_copy(outs_ref.at[pl.ds(0, 1)], buf.at[k, pl.ds(r, 1)], sem).wait()
        return 0

    lax.fori_loop(0, tb, drain, 0)

    y = (buf[0] + buf[1]) + (buf[2] + buf[3])
    m = mod_ref[0, 0]
    xnew = x_ref[...] + m[5:6] * y
    if final:
        xnew = _rms(xnew) * gfin_ref[...]
    o_ref[...] = xnew


def _combine(outs, pos, x_flat, modt, g_final, tiles_per_batch, n_ctx_tiles, final):
    n_tok, d = x_flat.shape
    tb = COMBINE_TILE
    n_tiles = n_tok // tb
    return pl.pallas_call(
        functools.partial(_combine_kernel, final=final),
        grid=(n_tiles,),
        in_specs=[pl.BlockSpec((1, 1, tb * TOP_K), lambda t: (t, 0, 0), memory_space=pltpu.SMEM),
                  pl.BlockSpec(memory_space=pl.ANY),
                  pl.BlockSpec((tb, d), lambda t: (t, 0)),
                  pl.BlockSpec((1, 1, 6, d), lambda t: (t // tiles_per_batch,
                                                        jnp.where(t % tiles_per_batch < n_ctx_tiles, 0, 1), 0, 0)),
                  pl.BlockSpec((1, d), lambda t: (0, 0))],
        out_specs=pl.BlockSpec((tb, d), lambda t: (t, 0)),
        out_shape=jax.ShapeDtypeStruct((n_tok, d), F32),
        scratch_shapes=[pltpu.VMEM((TOP_K, tb, d), F32), pltpu.SemaphoreType.DMA(())],
        compiler_params=pltpu.CompilerParams(dimension_semantics=("arbitrary",)),
        name="expert_combine",
    )(pos.reshape(n_tiles, 1, tb * TOP_K), outs, x_flat, modt, g_final.reshape(1, d))


def _dispatch_plan(top_idx, top_w, n_experts):
    n_tok = top_idx.shape[0]
    tm = EXPERT_TILE
    n_assign = n_tok * TOP_K
    flat_e = top_idx.reshape(-1)
    onehot = (flat_e[:, None] == jnp.arange(n_experts, dtype=jnp.int32)[None, :]).astype(jnp.int32)
    csum = jnp.cumsum(onehot, axis=0)
    rank = jnp.sum(csum * onehot, axis=1) - 1
    counts = csum[-1]
    padded = (counts + tm - 1) // tm * tm
    ends_p = jnp.cumsum(padded)
    starts_p = ends_p - padded
    dest = (starts_p[flat_e] + rank).astype(jnp.int32)
    n_blocks = (n_assign + n_experts * (tm - 1) + tm - 1) // tm
    n_rows = n_blocks * tm
    flat_tok = jnp.repeat(jnp.arange(n_tok, dtype=jnp.int32), TOP_K)
    row_tok = jnp.zeros((n_rows,), jnp.int32).at[dest].set(flat_tok, unique_indices=True)
    row_w = jnp.zeros((n_rows,), F32).at[dest].set(top_w.reshape(-1), unique_indices=True)
    block_start = jnp.arange(n_blocks, dtype=jnp.int32) * tm
    block_e = jnp.minimum(jnp.searchsorted(ends_p, block_start, side='right'), n_experts - 1).astype(jnp.int32)
    n_used = (ends_p[-1] // tm).astype(jnp.int32).reshape(1)
    return dest, row_tok, row_w, block_e, n_used


def _moe(tok_flat, top_idx, top_w, w_gu, b_gu, w_down, b_down):
    n_experts = w_gu.shape[0]
    dest, row_tok, row_w, block_e, n_used = _dispatch_plan(top_idx, top_w, n_experts)
    xs = _gather_rows(tok_flat, row_tok, n_used)
    h = _expert_up(xs, w_gu, b_gu, block_e, n_used)
    outs = _expert_down(h, w_down, b_down, row_w, block_e, n_used)
    return outs, dest


def kernel(x, c, ctx, c_ctx, w_mod, b_mod, w_in, conv_w, conv_b, w_r, b_r, w_i, b_i, lam, g_v, w_s, b_s,
           g_lru, g_mlp, w_out, w_router, b_router, w_gu, b_gu, w_down, b_down, g_final):
    bsz, n_lat, d = x.shape
    n_ctx = ctx.shape[1]
    depth = w_mod.shape[0]
    n_experts = w_router.shape[2]
    r = conv_w.shape[2]
    tt = TIME_TILE
    assert n_ctx % tt == 0 and n_lat % tt == 0 and tt % CHUNK == 0
    assert n_experts <= LANES and r % LRU_HEADS == 0
    s = n_ctx + n_lat
    n_ctx_tiles = n_ctx // tt

    rows = (bsz + 1 + SUBLANES - 1) // SUBLANES * SUBLANES
    c_all = jnp.zeros((rows, d), F32).at[:bsz].set(c).at[bsz].set(c_ctx)
    mod = _modulation(c_all, w_mod, b_mod)

    hcat = jnp.concatenate([ctx, x], axis=1)
    out = None
    for l in range(depth):
        last = l == depth - 1
        mod_x = mod[l, :bsz].reshape(bsz, 1, 6, d)
        mod_c = jnp.broadcast_to(mod[l, bsz].reshape(1, 1, 6, d), (bsz, 1, 6, d))
        modt = jnp.concatenate([mod_c, mod_x], axis=1)

        z = _inproj(hcat, modt, w_in[l].astype(BF16), n_ctx_tiles)
        hd = r // LRU_HEADS
        wg = jnp.concatenate([w_r[l], w_i[l]], axis=-1).astype(BF16)
        bg = jnp.concatenate([b_r[l], b_i[l]], axis=-1).reshape(2, LRU_HEADS, 1, 2 * hd)
        hf, hb = _scan(z, conv_w[l], conv_b[l].reshape(1, r), wg, bg, lam[l], n_ctx_tiles)

        wr_hi = w_router[l].astype(BF16)
        wr_lo = (w_router[l] - wr_hi.astype(F32)).astype(BF16)
        wr_split = jnp.pad(jnp.stack([wr_hi, wr_lo]), ((0, 0), (0, 0), (0, LANES - n_experts)))
        br_pad = jnp.pad(b_router[l], (0, LANES - n_experts)).reshape(1, LANES)
        off = n_ctx_tiles if last else 0
        xo, tok, idx, wt = _finish(
            z, hf, hb, hcat, modt, g_lru[l].reshape(1, r), g_mlp[l].reshape(1, r), g_v[l].reshape(1, r),
            w_s[l].astype(BF16), b_s[l].reshape(MLP_HEADS, CHUNK, 1), w_out[l].astype(BF16),
            wr_split, br_pad, n_experts, off, n_ctx_tiles)

        rows_out = xo.shape[1]
        n_tok = bsz * rows_out
        top_idx = idx.reshape(n_tok, LANES)[:, :TOP_K]
        top_w = wt.reshape(n_tok, LANES)[:, :TOP_K]
        outs, dest = _moe(tok.reshape(n_tok, d), top_idx, top_w, w_gu[l], b_gu[l], w_down[l], b_down[l])
        new = _combine(outs, dest, xo.reshape(n_tok, d), modt, g_final,
                       rows_out // COMBINE_TILE, 0 if last else n_ctx // COMBINE_TILE, last)
        new = new.reshape(bsz, rows_out, d)
        if last:
            out = new
        else:
            hcat = new
    return out
```

```python
import functools
import math

import jax
import jax.numpy as jnp
from jax import lax
from jax.experimental import pallas as pl
from jax.experimental.pallas import tpu as pltpu

LRU_HEADS = 8
MLP_HEADS = 8
CONV_W = 4
CONV_LEFT = 2
RG_C = 8.0
CHUNK = 128
TOP_K = 4
SWIGLU_LIMIT = 7.0
SWIGLU_ALPHA = 1.702
EPS = 1e-6

SUBLANES = 8
LANES = 128
TIME_TILE = 256
EXPERT_TILE = 512
DISPATCH_TILE = 128
COMBINE_TILE = 128
WAIT_GROUP = 64
VMEM_LIMIT = 56 * 1024 * 1024

F32 = jnp.float32
BF16 = jnp.bfloat16
U32 = jnp.uint32
HI_MASK = 0xFFFF0000


def _rms(x):
    return x * lax.rsqrt(jnp.mean(x * x, axis=-1, keepdims=True) + EPS)


def _gelu(x):
    c = math.sqrt(2.0 / math.pi)
    return 0.5 * x * (1.0 + jnp.tanh(c * (x + 0.044715 * (x * x * x))))


def _softplus(x):
    return jnp.maximum(x, 0.0) + jnp.log1p(jnp.exp(-jnp.abs(x)))


def _mod_kernel(c_ref, w_ref, b_ref, o_ref):
    c = c_ref[...]
    s = c * jax.nn.sigmoid(c)
    o_ref[0] = jnp.dot(s, w_ref[0], preferred_element_type=F32) + b_ref[0]


def _modulation(c_all, w_mod, b_mod):
    depth, d, n6 = w_mod.shape
    rows = c_all.shape[0]
    tn = n6 // 8
    return pl.pallas_call(
        _mod_kernel,
        grid=(depth, n6 // tn),
        in_specs=[
            pl.BlockSpec((rows, d), lambda l, n: (0, 0)),
            pl.BlockSpec((1, d, tn), lambda l, n: (l, 0, n)),
            pl.BlockSpec((1, 1, tn), lambda l, n: (l, 0, n)),
        ],
        out_specs=pl.BlockSpec((1, rows, tn), lambda l, n: (l, 0, n)),
        out_shape=jax.ShapeDtypeStruct((depth, rows, n6), F32),
        compiler_params=pltpu.CompilerParams(
            dimension_semantics=("arbitrary", "arbitrary"), vmem_limit_bytes=VMEM_LIMIT),
        name="modulation",
    )(c_all, w_mod, b_mod.reshape(depth, 1, n6))


def _inproj_kernel(x_ref, mod_ref, w_ref, z_ref, *, n_chunks):
    x = x_ref[0]
    m = mod_ref[0, 0]
    nx = _rms(x) * (1.0 + m[1:2]) + m[0:1]
    nb = nx.astype(BF16)
    cw = w_ref.shape[1] // n_chunks
    for c in range(n_chunks):
        z_ref[0, :, c * cw:(c + 1) * cw] = jnp.dot(
            nb, w_ref[:, c * cw:(c + 1) * cw], preferred_element_type=F32)


def _inproj(x, modt, w_in_bf16, n_ctx_tiles):
    bsz, s, d = x.shape
    n_cols = w_in_bf16.shape[1]
    tt = TIME_TILE
    return pl.pallas_call(
        functools.partial(_inproj_kernel, n_chunks=4),
        grid=(bsz, s // tt),
        in_specs=[
            pl.BlockSpec((1, tt, d), lambda b, i: (b, i, 0)),
            pl.BlockSpec((1, 1, 6, d), lambda b, i: (b, jnp.where(i < n_ctx_tiles, 0, 1), 0, 0)),
            pl.BlockSpec((d, n_cols), lambda b, i: (0, 0), pipeline_mode=pl.Buffered(1)),
        ],
        out_specs=pl.BlockSpec((1, tt, n_cols), lambda b, i: (b, i, 0)),
        out_shape=jax.ShapeDtypeStruct((bsz, s, n_cols), F32),
        compiler_params=pltpu.CompilerParams(
            dimension_semantics=("parallel", "arbitrary"), vmem_limit_bytes=VMEM_LIMIT),
        name="inproj",
    )(x, modt, w_in_bf16)


def _conv_tile(main, prev8, next8, cw, cb):
    tt = main.shape[0]
    row8 = lax.broadcasted_iota(jnp.int32, (SUBLANES, main.shape[1]), 0)
    w0, w1, w2, w3 = cw[0:1], cw[1:2], cw[2:3], cw[3:4]
    r1 = pltpu.roll(main, 1, 0)
    r2 = pltpu.roll(main, 2, 0)
    rm1 = pltpu.roll(main, tt - 1, 0)
    body = cb + w0 * r2 + w1 * r1 + w2 * main + w3 * rm1
    head = main[0:SUBLANES]
    h1 = jnp.where(row8 < 1, pltpu.roll(prev8, 1, 0), pltpu.roll(head, 1, 0))
    h2 = jnp.where(row8 < 2, pltpu.roll(prev8, 2, 0), pltpu.roll(head, 2, 0))
    head_out = cb + w0 * h2 + w1 * h1 + w2 * head + w3 * rm1[0:SUBLANES]
    tail_m1 = jnp.where(row8 == SUBLANES - 1, pltpu.roll(next8, SUBLANES - 1, 0), rm1[tt - SUBLANES:tt])
    tail_out = (cb + w0 * r2[tt - SUBLANES:tt] + w1 * r1[tt - SUBLANES:tt]
                + w2 * main[tt - SUBLANES:tt] + w3 * tail_m1)
    return body, head_out, tail_out


def _gates_tile(rc_ref, a_ref, u_ref, wg_ref, bg_ref, sp, direction):
    hd = rc_ref.shape[1] // LRU_HEADS
    for h in range(LRU_HEADS):
        cols = slice(h * hd, (h + 1) * hd)
        xh = rc_ref[:, cols]
        pre = jnp.dot(xh.astype(BF16), wg_ref[direction, h], preferred_element_type=F32) + bg_ref[direction, h]
        r = jax.nn.sigmoid(pre[:, :hd])
        gi = jax.nn.sigmoid(pre[:, hd:])
        a = jnp.exp((-RG_C) * r * sp[:, cols])
        a_ref[:, cols] = a
        u_ref[:, cols] = jnp.sqrt(1.0 - a * a) * gi * xh


def _scan_tile(a_ref, u_ref, out_ref, carry_ref, reverse):
    tt, width = a_ref.shape
    groups = tt // SUBLANES
    row8 = lax.broadcasted_iota(jnp.int32, (SUBLANES, width), 0)

    def body(g, carry):
        gi = (groups - 1 - g) if reverse else g
        sl = pl.ds(pl.multiple_of(gi * SUBLANES, SUBLANES), SUBLANES)
        a = a_ref[sl, :]
        u = u_ref[sl, :]
        for k in (1, 2, 4):
            if reverse:
                shift = SUBLANES - k
                valid = row8 < SUBLANES - k
            else:
                shift = k
                valid = row8 >= k
            a_s = pltpu.roll(a, shift, 0)
            u_s = pltpu.roll(u, shift, 0)
            u = jnp.where(valid, a * u_s + u, u)
            a = jnp.where(valid, a * a_s, a)
        h = a * carry + u
        out_ref[0, sl, :] = h
        return h[0:1] if reverse else h[SUBLANES - 1:SUBLANES]

    carry_ref[...] = lax.fori_loop(0, groups, body, carry_ref[...])


def _scan_kernel(zf_ref, zfp_ref, zfn_ref, zb_ref, zbp_ref, zbn_ref, cw_ref, cb_ref, wg_ref, bg_ref,
                 lam_ref, hf_ref, hb_ref, rc_ref, a_ref, u_ref, cf_ref, cbk_ref, *, n_ctx_tiles, n_tiles):
    i = pl.program_id(1)
    tt = rc_ref.shape[0]

    @pl.when(i == 0)
    def _():
        cf_ref[...] = jnp.zeros_like(cf_ref)
        cbk_ref[...] = jnp.zeros_like(cbk_ref)

    jb = jnp.where(i < n_ctx_tiles, n_ctx_tiles - 1 - i, n_tiles - 1 - (i - n_ctx_tiles))
    cw = cw_ref[...]
    cb = cb_ref[...]
    sp = _softplus(-lam_ref[...])

    def run(z_ref, zp_ref, zn_ref, j, direction, out_ref, carry_ref):
        first = jnp.logical_or(j == 0, j == n_ctx_tiles)
        last = jnp.logical_or(j == n_ctx_tiles - 1, j == n_tiles - 1)
        prev8 = zp_ref[0] * jnp.where(first, 0.0, 1.0)
        next8 = zn_ref[0] * jnp.where(last, 0.0, 1.0)
        body, head_out, tail_out = _conv_tile(z_ref[0], prev8, next8, cw, cb)
        rc_ref[...] = body
        rc_ref[0:SUBLANES] = head_out
        rc_ref[tt - SUBLANES:tt] = tail_out
        _gates_tile(rc_ref, a_ref, u_ref, wg_ref, bg_ref, sp[direction:direction + 1], direction)
        _scan_tile(a_ref, u_ref, out_ref, carry_ref, reverse=(direction == 1))

    run(zf_ref, zfp_ref, zfn_ref, i, 0, hf_ref, cf_ref)
    run(zb_ref, zbp_ref, zbn_ref, jb, 1, hb_ref, cbk_ref)


def _scan(z, conv_w, conv_b, wg, bg, lam, n_ctx_tiles):
    bsz, s, _ = z.shape
    r = conv_w.shape[1]
    tt = TIME_TILE
    n_tiles = s // tt
    per = tt // SUBLANES
    last8 = s // SUBLANES - 1

    def bwd_tile(i):
        return jnp.where(i < n_ctx_tiles, n_ctx_tiles - 1 - i, n_tiles - 1 - (i - n_ctx_tiles))

    main_f = pl.BlockSpec((1, tt, r), lambda b, i: (b, i, 0))
    prev_f = pl.BlockSpec((1, SUBLANES, r), lambda b, i: (b, jnp.maximum(i * per - 1, 0), 0))
    next_f = pl.BlockSpec((1, SUBLANES, r), lambda b, i: (b, jnp.minimum((i + 1) * per, last8), 0))
    main_b = pl.BlockSpec((1, tt, r), lambda b, i: (b, bwd_tile(i), 0))
    prev_b = pl.BlockSpec((1, SUBLANES, r), lambda b, i: (b, jnp.maximum(bwd_tile(i) * per - 1, 0), 0))
    next_b = pl.BlockSpec((1, SUBLANES, r), lambda b, i: (b, jnp.minimum((bwd_tile(i) + 1) * per, last8), 0))
    full = lambda shape: pl.BlockSpec(shape, lambda b, i: (0,) * len(shape))
    return pl.pallas_call(
        functools.partial(_scan_kernel, n_ctx_tiles=n_ctx_tiles, n_tiles=n_tiles),
        grid=(bsz, n_tiles),
        in_specs=[main_f, prev_f, next_f, main_b, prev_b, next_b,
                  full(conv_w.shape), full(conv_b.shape), full(wg.shape), full(bg.shape), full(lam.shape)],
        out_specs=[pl.BlockSpec((1, tt, r), lambda b, i: (b, i, 0)),
                   pl.BlockSpec((1, tt, r), lambda b, i: (b, bwd_tile(i), 0))],
        out_shape=[jax.ShapeDtypeStruct((bsz, s, r), F32), jax.ShapeDtypeStruct((bsz, s, r), F32)],
        scratch_shapes=[pltpu.VMEM((tt, r), F32), pltpu.VMEM((tt, r), F32), pltpu.VMEM((tt, r), F32),
                        pltpu.VMEM((1, r), F32), pltpu.VMEM((1, r), F32)],
        compiler_params=pltpu.CompilerParams(
            dimension_semantics=("parallel", "arbitrary"), vmem_limit_bytes=VMEM_LIMIT),
        name="lru_scan",
    )(z, z, z, z, z, z, conv_w, conv_b, wg, bg, lam)


def _finish_kernel(zg_ref, zu_ref, zv_ref, hf_ref, hb_ref, x_ref, mod_ref, glru_ref, gmlp_ref, gv_ref,
                   ws_ref, bs_ref, wout_ref, wr_ref, br_ref,
                   xo_ref, tokp_ref, route_ref, wt_ref, cnt_ref, ym_ref, run_ref, *, n_experts):
    tt, r = ym_ref.shape
    hd = r // MLP_HEADS

    @pl.when(jnp.logical_and(pl.program_id(0) == 0, pl.program_id(1) == 0))
    def _():
        run_ref[...] = jnp.zeros_like(run_ref)

    y_lru = (hf_ref[0] + hb_ref[0]) * _gelu(zg_ref[0])
    yl = _rms(y_lru) * glru_ref[...]
    u = _gelu(zu_ref[0])
    vb = (_rms(_gelu(zv_ref[0])) * gv_ref[...]).astype(BF16)
    for ch in range(tt // CHUNK):
        rows = slice(ch * CHUNK, (ch + 1) * CHUNK)
        for h in range(MLP_HEADS):
            cols = slice(h * hd, (h + 1) * hd)
            mixed = jnp.dot(ws_ref[h], vb[rows, cols], preferred_element_type=F32) + bs_ref[h]
            ym_ref[rows, cols] = u[rows, cols] * mixed
    ym = _rms(ym_ref[...]) * gmlp_ref[...]
    y = (jnp.dot(yl.astype(BF16), wout_ref[0:r], preferred_element_type=F32)
         + jnp.dot(ym.astype(BF16), wout_ref[r:2 * r], preferred_element_type=F32))
    m = mod_ref[0, 0]
    xnew = x_ref[0] + m[2:3] * y
    xo_ref[0] = xnew
    tok = _rms(xnew) * (1.0 + m[4:5]) + m[3:4]

    half = tok.shape[1] // 2
    t_hi = tok.astype(BF16)
    t_hi32 = t_hi.astype(F32)
    bits = lax.bitcast_convert_type(t_hi32, U32)
    tokp_ref[0] = bits[:, :half] | (bits[:, half:] >> 16)

    t_lo = (tok - t_hi32).astype(BF16)
    logits = (jnp.dot(t_hi, wr_ref[0], preferred_element_type=F32)
              + jnp.dot(t_lo, wr_ref[0], preferred_element_type=F32)
              + jnp.dot(t_hi, wr_ref[1], preferred_element_type=F32)) + br_ref[...]
    lane = lax.broadcasted_iota(jnp.int32, logits.shape, 1)
    neg = jnp.float32(-jnp.inf)
    work = jnp.where(lane < n_experts, logits, neg)
    vals, idxs = [], []
    for _ in range(TOP_K):
        mx = jnp.max(work, axis=-1, keepdims=True)
        ix = jnp.min(jnp.where(work == mx, lane, LANES), axis=-1, keepdims=True)
        vals.append(mx)
        idxs.append(ix)
        work = jnp.where(lane == ix, neg, work)
    exps = [jnp.exp(v - vals[0]) for v in vals]
    denom = exps[0] + exps[1] + exps[2] + exps[3]

    hot = [lane == ix for ix in idxs]
    multi = jnp.zeros(logits.shape, F32)
    for k in range(TOP_K):
        multi = multi + hot[k].astype(F32)
    before = (lax.broadcasted_iota(jnp.int32, (tt, tt), 1) < lax.broadcasted_iota(jnp.int32, (tt, tt), 0))
    excl = jnp.dot(before.astype(BF16), multi.astype(BF16), preferred_element_type=F32) + run_ref[...]
    run_ref[...] = run_ref[...] + jnp.sum(multi, axis=0, keepdims=True)
    cnt_ref[...] = jnp.broadcast_to(run_ref[...], cnt_ref.shape).astype(jnp.int32)

    route = jnp.zeros(logits.shape, jnp.int32)
    wt_out = jnp.zeros(logits.shape, F32)
    for k in range(TOP_K):
        rank = jnp.sum(jnp.where(hot[k], excl, 0.0), axis=-1, keepdims=True).astype(jnp.int32)
        route = jnp.where(lane == k, idxs[k], route)
        route = jnp.where(lane == TOP_K + k, rank, route)
        wt_out = jnp.where(lane == k, exps[k] / denom, wt_out)
    route_ref[0] = route
    wt_ref[0] = wt_out


def _finish(z, hf, hb, x, modt, g_lru, g_mlp, g_v, ws, bs, w_out_bf16, wr_split, br_pad, n_experts,
            tile_offset, n_ctx_tiles):
    bsz, s, d = x.shape
    r = hf.shape[2]
    tt = TIME_TILE
    n_out_tiles = s // tt - tile_offset
    rows_out = n_out_tiles * tt
    off = tile_offset

    def zcol(c):
        return pl.BlockSpec((1, tt, r), lambda b, i: (b, i + off, c))

    full = lambda shape: pl.BlockSpec(shape, lambda b, i: (0,) * len(shape))
    tile_in = lambda w: pl.BlockSpec((1, tt, w), lambda b, i: (b, i + off, 0))
    tile_out = lambda w: pl.BlockSpec((1, tt, w), lambda b, i: (b, i, 0))
    return pl.pallas_call(
        functools.partial(_finish_kernel, n_experts=n_experts),
        grid=(bsz, n_out_tiles),
        in_specs=[zcol(1), zcol(2), zcol(3), tile_in(r), tile_in(r), tile_in(d),
                  pl.BlockSpec((1, 1, 6, d), lambda b, i: (b, jnp.where(i + off < n_ctx_tiles, 0, 1), 0, 0)),
                  full(g_lru.shape), full(g_mlp.shape), full(g_v.shape), full(ws.shape), full(bs.shape),
                  pl.BlockSpec(w_out_bf16.shape, lambda b, i: (0, 0), pipeline_mode=pl.Buffered(1)),
                  full(wr_split.shape), full(br_pad.shape)],
        out_specs=[tile_out(d), tile_out(d // 2), tile_out(LANES), tile_out(LANES),
                   pl.BlockSpec((SUBLANES, LANES), lambda b, i: (0, 0))],
        out_shape=[jax.ShapeDtypeStruct((bsz, rows_out, d), F32),
                   jax.ShapeDtypeStruct((bsz, rows_out, d // 2), U32),
                   jax.ShapeDtypeStruct((bsz, rows_out, LANES), jnp.int32),
                   jax.ShapeDtypeStruct((bsz, rows_out, LANES), F32),
                   jax.ShapeDtypeStruct((SUBLANES, LANES), jnp.int32)],
        scratch_shapes=[pltpu.VMEM((tt, r), F32), pltpu.VMEM((1, LANES), F32)],
        compiler_params=pltpu.CompilerParams(
            dimension_semantics=("arbitrary", "arbitrary"), vmem_limit_bytes=VMEM_LIMIT),
        name="mixer_finish",
    )(z, z, z, hf, hb, x, modt, g_lru, g_mlp, g_v, ws, bs, w_out_bf16, wr_split, br_pad)


def _row_copy(src_ref, src_row, dst_ref, dst_row, sem):
    return pltpu.make_async_copy(src_ref.at[pl.ds(src_row, 1)], dst_ref.at[pl.ds(dst_row, 1)], sem)


def _wait_rows(src_ref, dst_ref, sem, n_rows):
    def group(_, carry):
        for _ in range(WAIT_GROUP):
            _row_copy(src_ref, 0, dst_ref, 0, sem).wait()
        return carry

    lax.fori_loop(0, n_rows // WAIT_GROUP, group, 0)


def _zero_pieces(tile):
    return [1 << b for b in range(tile.bit_length() - 2, 2, -1)]


def _dispatch_kernel(starts_ref, zstart_ref, zlen_ref, idx_ref, rank_ref, tok_ref, xs_ref, zbuf, sem, zsem,
                     *, n_experts):
    s = pl.program_id(0)
    ns = pl.num_programs(0)
    td = idx_ref.shape[2] // TOP_K
    pieces = _zero_pieces(EXPERT_TILE)

    @pl.when(s == 0)
    def _():
        zbuf[...] = jnp.zeros_like(zbuf)

        def fill(e, carry):
            gap = zlen_ref[e]
            start = zstart_ref[e]
            lead = jnp.minimum((SUBLANES - (start & (SUBLANES - 1))) & (SUBLANES - 1), gap)
            for i in range(SUBLANES - 1):
                @pl.when(i < lead)
                def _():
                    _row_copy(zbuf, 0, xs_ref, start + i, zsem).start()
            rest = gap - lead
            off = start + lead
            for p in pieces:
                @pl.when((rest & p) != 0)
                def _():
                    pltpu.make_async_copy(zbuf.at[pl.ds(0, p)],
                                          xs_ref.at[pl.ds(pl.multiple_of(off, SUBLANES), p)], zsem).start()

                off = off + (rest & p)
            for i in range(SUBLANES - 1):
                @pl.when(i < lead)
                def _():
                    _row_copy(zbuf, 0, xs_ref, 0, zsem).wait()
            for p in pieces:
                @pl.when((rest & p) != 0)
                def _():
                    pltpu.make_async_copy(zbuf.at[pl.ds(0, p)], xs_ref.at[pl.ds(0, p)], zsem).wait()
            return carry

        lax.fori_loop(0, n_experts, fill, 0)

        zrows = zbuf.shape[0]
        first_free = (zstart_ref[n_experts - 1] + zlen_ref[n_experts - 1]) // zrows
        n_chunks = xs_ref.shape[0] // zrows

        def fill_tail(c, carry):
            row = pl.multiple_of(c * zrows, zrows)
            cp = pltpu.make_async_copy(zbuf, xs_ref.at[pl.ds(row, zrows)], zsem)
            cp.start()
            cp.wait()
            return carry

        lax.fori_loop(first_free, n_chunks, fill_tail, 0)

    slot = s % 2

    def issue(t, carry):
        for k in range(TOP_K):
            e = idx_ref[0, 0, t * TOP_K + k]
            d = starts_ref[e] + rank_ref[0, 0, t * TOP_K + k]
            _row_copy(tok_ref, s * td + t, xs_ref, d, sem.at[slot]).start()
        return carry

    lax.fori_loop(0, td, issue, 0, unroll=2)

    @pl.when(s > 0)
    def _():
        _wait_rows(tok_ref, xs_ref, sem.at[1 - slot], td * TOP_K)

    @pl.when(s == ns - 1)
    def _():
        _wait_rows(tok_ref, xs_ref, sem.at[slot], td * TOP_K)


def _dispatch(tokp, idx4, rank4, starts_p, zstart, zlen, n_rows):
    n_tok, dh = tokp.shape
    td = DISPATCH_TILE
    n_steps = n_tok // td
    n_experts = starts_p.shape[0]
    zrows = _zero_pieces(EXPERT_TILE)[0]
    return pl.pallas_call(
        functools.partial(_dispatch_kernel, n_experts=n_experts),
        grid_spec=pltpu.PrefetchScalarGridSpec(
            num_scalar_prefetch=3,
            grid=(n_steps,),
            in_specs=[pl.BlockSpec((1, 1, td * TOP_K), lambda s, a, b, c: (s, 0, 0), memory_space=pltpu.SMEM),
                      pl.BlockSpec((1, 1, td * TOP_K), lambda s, a, b, c: (s, 0, 0), memory_space=pltpu.SMEM),
                      pl.BlockSpec(memory_space=pl.ANY)],
            out_specs=pl.BlockSpec(memory_space=pl.ANY),
            scratch_shapes=[pltpu.VMEM((zrows, dh), U32), pltpu.SemaphoreType.DMA((2,)),
                            pltpu.SemaphoreType.DMA(())],
        ),
        out_shape=jax.ShapeDtypeStruct((n_rows, dh), U32),
        compiler_params=pltpu.CompilerParams(dimension_semantics=("arbitrary",)),
        name="expert_dispatch",
    )(starts_p, zstart, zlen, idx4.reshape(n_steps, 1, td * TOP_K), rank4.reshape(n_steps, 1, td * TOP_K), tokp)


def _expert_up_kernel(be_ref, nused_ref, x_ref, wg_ref, wu_ref, bgate_ref, bup_ref, h_ref, wg_s, wu_s):
    j = pl.program_id(1)
    changed = jnp.logical_or(j == 0, be_ref[j] != be_ref[jnp.maximum(j - 1, 0)])

    @pl.when(changed)
    def _():
        wg_s[...] = wg_ref[0, 0].astype(BF16)
        wu_s[...] = wu_ref[0, 0].astype(BF16)

    @pl.when(j < nused_ref[0])
    def _():
        xp = x_ref[...]
        half = xp.shape[1]
        xa = lax.bitcast_convert_type(xp & jnp.uint32(HI_MASK), F32).astype(BF16)
        xb = lax.bitcast_convert_type(xp << 16, F32).astype(BF16)
        gate = (jnp.dot(xa, wg_s[0:half], preferred_element_type=F32)
                + jnp.dot(xb, wg_s[half:2 * half], preferred_element_type=F32)) + bgate_ref[0, 0]
        up = (jnp.dot(xa, wu_s[0:half], preferred_element_type=F32)
              + jnp.dot(xb, wu_s[half:2 * half], preferred_element_type=F32)) + bup_ref[0, 0]
        gate = jnp.minimum(gate, SWIGLU_LIMIT)
        up = jnp.clip(up, -SWIGLU_LIMIT, SWIGLU_LIMIT)
        act = (up + 1.0) * gate * jax.nn.sigmoid(SWIGLU_ALPHA * gate)
        h_ref[...] = act.astype(BF16)

    @pl.when(j >= nused_ref[0])
    def _():
        h_ref[...] = jnp.zeros_like(h_ref)


def _expert_up(xs, w_gu, b_gu, layer, block_e, n_used):
    n_rows, dh = xs.shape
    d = 2 * dh
    _, n_exp, _, two_f = w_gu.shape
    f = two_f // 2
    tm = EXPERT_TILE
    tn = 512
    n_blocks = n_rows // tm
    nf = f // tn
    b4 = b_gu.reshape(b_gu.shape[0], n_exp, 1, two_f)
    return pl.pallas_call(
        _expert_up_kernel,
        grid_spec=pltpu.PrefetchScalarGridSpec(
            num_scalar_prefetch=2,
            grid=(nf, n_blocks),
            in_specs=[pl.BlockSpec((tm, dh), lambda n, j, be, nu: (j, 0)),
                      pl.BlockSpec((1, 1, d, tn), lambda n, j, be, nu: (layer, be[j], 0, n)),
                      pl.BlockSpec((1, 1, d, tn), lambda n, j, be, nu: (layer, be[j], 0, nf + n)),
                      pl.BlockSpec((1, 1, 1, tn), lambda n, j, be, nu: (layer, be[j], 0, n)),
                      pl.BlockSpec((1, 1, 1, tn), lambda n, j, be, nu: (layer, be[j], 0, nf + n))],
            out_specs=pl.BlockSpec((tm, tn), lambda n, j, be, nu: (j, n)),
            scratch_shapes=[pltpu.VMEM((d, tn), BF16), pltpu.VMEM((d, tn), BF16)],
        ),
        out_shape=jax.ShapeDtypeStruct((n_rows, f), BF16),
        compiler_params=pltpu.CompilerParams(
            dimension_semantics=("arbitrary", "arbitrary"), vmem_limit_bytes=VMEM_LIMIT),
        name="expert_up",
    )(block_e, n_used, xs, w_gu, w_gu, b4, b4)


def _expert_down_kernel(be_ref, nused_ref, h_ref, wd_ref, bd_ref, o_ref, wd_s):
    j = pl.program_id(1)
    changed = jnp.logical_or(j == 0, be_ref[j] != be_ref[jnp.maximum(j - 1, 0)])

    @pl.when(changed)
    def _():
        wd_s[...] = wd_ref[0, 0].astype(BF16)

    @pl.when(j < nused_ref[0])
    def _():
        o_ref[...] = jnp.dot(h_ref[...], wd_s[...], preferred_element_type=F32) + bd_ref[0, 0]

    @pl.when(j >= nused_ref[0])
    def _():
        o_ref[...] = jnp.zeros_like(o_ref)


def _expert_down(h, w_down, b_down, layer, block_e, n_used):
    n_rows, f = h.shape
    _, n_exp, _, d = w_down.shape
    tm = EXPERT_TILE
    tn = 1024
    n_blocks = n_rows // tm
    return pl.pallas_call(
        _expert_down_kernel,
        grid_spec=pltpu.PrefetchScalarGridSpec(
            num_scalar_prefetch=2,
            grid=(d // tn, n_blocks),
            in_specs=[pl.BlockSpec((tm, f), lambda n, j, be, nu: (j, 0)),
                      pl.BlockSpec((1, 1, f, tn), lambda n, j, be, nu: (layer, be[j], 0, n)),
                      pl.BlockSpec((1, 1, 1, tn), lambda n, j, be, nu: (layer, be[j], 0, n))],
            out_specs=pl.BlockSpec((tm, tn), lambda n, j, be, nu: (j, n)),
            scratch_shapes=[pltpu.VMEM((f, tn), BF16)],
        ),
        out_shape=jax.ShapeDtypeStruct((n_rows, d), F32),
        compiler_params=pltpu.CompilerParams(
            dimension_semantics=("arbitrary", "arbitrary"), vmem_limit_bytes=VMEM_LIMIT),
        name="expert_down",
    )(block_e, n_used, h, w_down, b_down.reshape(b_down.shape[0], n_exp, 1, d))


def _combine_kernel(starts_ref, idxc_ref, rankc_ref, idxn_ref, rankn_ref, outs_ref, w_ref, x_ref, mod_ref,
                    gfin_ref, o_ref, buf, sem, *, final):
    t = pl.program_id(0)
    nt = pl.num_programs(0)
    tb = x_ref.shape[0]

    def issue(idx_ref, rank_ref, slot):
        def body(r, carry):
            for k in range(TOP_K):
                e = idx_ref[0, 0, r * TOP_K + k]
                p = starts_ref[e] + rank_ref[0, 0, r * TOP_K + k]
                pltpu.make_async_copy(outs_ref.at[pl.ds(p, 1)], buf.at[slot, k, pl.ds(r, 1)], sem.at[slot]).start()
            return carry

        lax.fori_loop(0, tb, body, 0, unroll=2)

    @pl.when(t == 0)
    def _():
        issue(idxc_ref, rankc_ref, 0)

    @pl.when(t + 1 < nt)
    def _():
        issue(idxn_ref, rankn_ref, (t + 1) % 2)

    slot = t % 2

    def group(_, carry):
        for _ in range(WAIT_GROUP):
            pltpu.make_async_copy(outs_ref.at[pl.ds(0, 1)], buf.at[slot, 0, pl.ds(0, 1)], sem.at[slot]).wait()
        return carry

    lax.fori_loop(0, tb * TOP_K // WAIT_GROUP, group, 0)

    w = w_ref[...]
    y = w[:, 0:1] * buf[slot, 0]
    for k in range(1, TOP_K):
        y = y + w[:, k:k + 1] * buf[slot, k]
    m = mod_ref[0, 0]
    xnew = x_ref[...] + m[5:6] * y
    if final:
        xnew = _rms(xnew) * gfin_ref[...]
    o_ref[...] = xnew


def _combine(outs, idx4, rank4, wt, starts_p, x_flat, modt, g_final, tiles_per_batch, n_ctx_tiles, final):
    n_tok, d = x_flat.shape
    tb = COMBINE_TILE
    n_tiles = n_tok // tb
    cur = lambda t, sp: (t, 0, 0)
    nxt = lambda t, sp: (jnp.minimum(t + 1, n_tiles - 1), 0, 0)
    smem = lambda imap: pl.BlockSpec((1, 1, tb * TOP_K), imap, memory_space=pltpu.SMEM)
    idx3 = idx4.reshape(n_tiles, 1, tb * TOP_K)
    rank3 = rank4.reshape(n_tiles, 1, tb * TOP_K)
    return pl.pallas_call(
        functools.partial(_combine_kernel, final=final),
        grid_spec=pltpu.PrefetchScalarGridSpec(
            num_scalar_prefetch=1,
            grid=(n_tiles,),
            in_specs=[smem(cur), smem(cur), smem(nxt), smem(nxt),
                      pl.BlockSpec(memory_space=pl.ANY),
                      pl.BlockSpec((tb, LANES), lambda t, sp: (t, 0)),
                      pl.BlockSpec((tb, d), lambda t, sp: (t, 0)),
                      pl.BlockSpec((1, 1, 6, d), lambda t, sp: (
                          t // tiles_per_batch, jnp.where(t % tiles_per_batch < n_ctx_tiles, 0, 1), 0, 0)),
                      pl.BlockSpec((1, d), lambda t, sp: (0, 0))],
            out_specs=pl.BlockSpec((tb, d), lambda t, sp: (t, 0)),
            scratch_shapes=[pltpu.VMEM((2, TOP_K, tb, d), F32), pltpu.SemaphoreType.DMA((2,))],
        ),
        out_shape=jax.ShapeDtypeStruct((n_tok, d), F32),
        compiler_params=pltpu.CompilerParams(
            dimension_semantics=("arbitrary",), vmem_limit_bytes=VMEM_LIMIT),
        name="expert_combine",
    )(starts_p, idx3, rank3, idx3, rank3, outs, wt, x_flat, modt, g_final.reshape(1, d))


def _expert_layout(counts, n_assign):
    n_experts = counts.shape[0]
    tm = EXPERT_TILE
    padded = (counts + tm - 1) // tm * tm
    ends_p = jnp.cumsum(padded).astype(jnp.int32)
    starts_p = ends_p - padded
    n_blocks = (n_assign + n_experts * (tm - 1) + tm - 1) // tm
    block_start = jnp.arange(n_blocks, dtype=jnp.int32) * tm
    block_e = jnp.minimum(jnp.sum((ends_p[None, :] <= block_start[:, None]).astype(jnp.int32), axis=1),
                          n_experts - 1).astype(jnp.int32)
    n_used = (ends_p[-1] // tm).astype(jnp.int32).reshape(1)
    return starts_p, starts_p + counts, padded - counts, block_e, n_used, n_blocks * tm


def _moe(tokp, route, counts, w_gu, b_gu, w_down, b_down, layer):
    n_tok = tokp.shape[0]
    idx4 = route[:, :TOP_K]
    rank4 = route[:, TOP_K:2 * TOP_K]
    starts_p, zstart, zlen, block_e, n_used, n_rows = _expert_layout(counts, n_tok * TOP_K)
    xs = _dispatch(tokp, idx4, rank4, starts_p, zstart, zlen, n_rows)
    h = _expert_up(xs, w_gu, b_gu, layer, block_e, n_used)
    outs = _expert_down(h, w_down, b_down, layer, block_e, n_used)
    return outs, idx4, rank4, starts_p


def kernel(x, c, ctx, c_ctx, w_mod, b_mod, w_in, conv_w, conv_b, w_r, b_r, w_i, b_i, lam, g_v, w_s, b_s,
           g_lru, g_mlp, w_out, w_router, b_router, w_gu, b_gu, w_down, b_down, g_final):
    bsz, n_lat, d = x.shape
    n_ctx = ctx.shape[1]
    depth = w_mod.shape[0]
    n_experts = w_router.shape[2]
    r = conv_w.shape[2]
    tt = TIME_TILE
    assert n_ctx % tt == 0 and n_lat % tt == 0 and tt % CHUNK == 0
    assert tt % DISPATCH_TILE == 0 and tt % COMBINE_TILE == 0
    assert n_experts <= LANES and r % LRU_HEADS == 0
    n_ctx_tiles = n_ctx // tt

    rows = (bsz + 1 + SUBLANES - 1) // SUBLANES * SUBLANES
    c_all = jnp.zeros((rows, d), F32).at[:bsz].set(c).at[bsz].set(c_ctx)
    mod = _modulation(c_all, w_mod, b_mod)

    hcat = jnp.concatenate([ctx, x], axis=1)
    out = None
    for l in range(depth):
        last = l == depth - 1
        mod_x = mod[l, :bsz].reshape(bsz, 1, 6, d)
        mod_c = jnp.broadcast_to(mod[l, bsz].reshape(1, 1, 6, d), (bsz, 1, 6, d))
        modt = jnp.concatenate([mod_c, mod_x], axis=1)

        z = _inproj(hcat, modt, w_in[l].astype(BF16), n_ctx_tiles)
        hd = r // LRU_HEADS
        wg = jnp.concatenate([w_r[l], w_i[l]], axis=-1).astype(BF16)
        bg = jnp.concatenate([b_r[l], b_i[l]], axis=-1).reshape(2, LRU_HEADS, 1, 2 * hd)
        hf, hb = _scan(z, conv_w[l], conv_b[l].reshape(1, r), wg, bg, lam[l], n_ctx_tiles)

        wr_hi = w_router[l].astype(BF16)
        wr_lo = (w_router[l] - wr_hi.astype(F32)).astype(BF16)
        wr_split = jnp.pad(jnp.stack([wr_hi, wr_lo]), ((0, 0), (0, 0), (0, LANES - n_experts)))
        br_pad = jnp.pad(b_router[l], (0, LANES - n_experts)).reshape(1, LANES)
        off = n_ctx_tiles if last else 0
        xo, tokp, route, wt, cnt = _finish(
            z, hf, hb, hcat, modt, g_lru[l].reshape(1, r), g_mlp[l].reshape(1, r), g_v[l].reshape(1, r),
            w_s[l].astype(BF16), b_s[l].reshape(MLP_HEADS, CHUNK, 1), w_out[l].astype(BF16),
            wr_split, br_pad, n_experts, off, n_ctx_tiles)

        rows_out = xo.shape[1]
        n_tok = bsz * rows_out
        outs, idx4, rank4, starts_p = _moe(
            tokp.reshape(n_tok, d // 2), route.reshape(n_tok, LANES), cnt[0, :n_experts],
            w_gu, b_gu, w_down, b_down, l)
        new = _combine(outs, idx4, rank4, wt.reshape(n_tok, LANES), starts_p, xo.reshape(n_tok, d), modt,
                       g_final, rows_out // COMBINE_TILE, 0 if last else n_ctx // COMBINE_TILE, last)
        new = new.reshape(bsz, rows_out, d)
        if last:
            out = new
        else:
            hcat = new
    return out
```

```python
import functools
import math

import jax
import jax.numpy as jnp
from jax import lax
from jax.experimental import pallas as pl
from jax.experimental.pallas import tpu as pltpu

LRU_HEADS = 8
MLP_HEADS = 8
CONV_W = 4
CONV_LEFT = 2
RG_C = 8.0
CHUNK = 128
TOP_K = 4
SWIGLU_LIMIT = 7.0
SWIGLU_ALPHA = 1.702
EPS = 1e-6

SUBLANES = 8
LANES = 128
TIME_TILE = 256
EXPERT_TILE = 512
DISPATCH_TILE = 128
COMBINE_TILE = 128
WAIT_GROUP = 64
UP_COLS = 1024
UP_COL_CHUNKS = 2
VMEM_LIMIT = 56 * 1024 * 1024

F32 = jnp.float32
BF16 = jnp.bfloat16
U32 = jnp.uint32
HI_MASK = 0xFFFF0000


def _rms(x):
    return x * lax.rsqrt(jnp.mean(x * x, axis=-1, keepdims=True) + EPS)


def _gelu(x):
    c = math.sqrt(2.0 / math.pi)
    return 0.5 * x * (1.0 + jnp.tanh(c * (x + 0.044715 * (x * x * x))))


def _softplus(x):
    return jnp.maximum(x, 0.0) + jnp.log1p(jnp.exp(-jnp.abs(x)))


def _mod_kernel(c_ref, w_ref, b_ref, o_ref):
    c = c_ref[...]
    s = c * jax.nn.sigmoid(c)
    o_ref[0] = jnp.dot(s, w_ref[0], preferred_element_type=F32) + b_ref[0]


def _modulation(c_all, w_mod, b_mod):
    depth, d, n6 = w_mod.shape
    rows = c_all.shape[0]
    tn = n6 // 8
    return pl.pallas_call(
        _mod_kernel,
        grid=(depth, n6 // tn),
        in_specs=[
            pl.BlockSpec((rows, d), lambda l, n: (0, 0)),
            pl.BlockSpec((1, d, tn), lambda l, n: (l, 0, n)),
            pl.BlockSpec((1, 1, tn), lambda l, n: (l, 0, n)),
        ],
        out_specs=pl.BlockSpec((1, rows, tn), lambda l, n: (l, 0, n)),
        out_shape=jax.ShapeDtypeStruct((depth, rows, n6), F32),
        compiler_params=pltpu.CompilerParams(
            dimension_semantics=("arbitrary", "arbitrary"), vmem_limit_bytes=VMEM_LIMIT),
        name="modulation",
    )(c_all, w_mod, b_mod.reshape(depth, 1, n6))


def _inproj_kernel(x_ref, mod_ref, w_ref, z_ref, *, n_chunks):
    x = x_ref[0]
    m = mod_ref[0, 0]
    nx = _rms(x) * (1.0 + m[1:2]) + m[0:1]
    nb = nx.astype(BF16)
    cw = w_ref.shape[1] // n_chunks
    for c in range(n_chunks):
        z_ref[0, :, c * cw:(c + 1) * cw] = jnp.dot(
            nb, w_ref[:, c * cw:(c + 1) * cw], preferred_element_type=F32)


def _inproj(x, modt, w_in_bf16, n_ctx_tiles):
    bsz, s, d = x.shape
    n_cols = w_in_bf16.shape[1]
    tt = TIME_TILE
    return pl.pallas_call(
        functools.partial(_inproj_kernel, n_chunks=4),
        grid=(bsz, s // tt),
        in_specs=[
            pl.BlockSpec((1, tt, d), lambda b, i: (b, i, 0)),
            pl.BlockSpec((1, 1, 6, d), lambda b, i: (b, jnp.where(i < n_ctx_tiles, 0, 1), 0, 0)),
            pl.BlockSpec((d, n_cols), lambda b, i: (0, 0), pipeline_mode=pl.Buffered(1)),
        ],
        out_specs=pl.BlockSpec((1, tt, n_cols), lambda b, i: (b, i, 0)),
        out_shape=jax.ShapeDtypeStruct((bsz, s, n_cols), F32),
        compiler_params=pltpu.CompilerParams(
            dimension_semantics=("parallel", "arbitrary"), vmem_limit_bytes=VMEM_LIMIT),
        name="inproj",
    )(x, modt, w_in_bf16)


def _conv_tile(main, prev8, next8, cw, cb):
    tt = main.shape[0]
    row8 = lax.broadcasted_iota(jnp.int32, (SUBLANES, main.shape[1]), 0)
    w0, w1, w2, w3 = cw[0:1], cw[1:2], cw[2:3], cw[3:4]
    r1 = pltpu.roll(main, 1, 0)
    r2 = pltpu.roll(main, 2, 0)
    rm1 = pltpu.roll(main, tt - 1, 0)
    body = cb + w0 * r2 + w1 * r1 + w2 * main + w3 * rm1
    head = main[0:SUBLANES]
    h1 = jnp.where(row8 < 1, pltpu.roll(prev8, 1, 0), pltpu.roll(head, 1, 0))
    h2 = jnp.where(row8 < 2, pltpu.roll(prev8, 2, 0), pltpu.roll(head, 2, 0))
    head_out = cb + w0 * h2 + w1 * h1 + w2 * head + w3 * rm1[0:SUBLANES]
    tail_m1 = jnp.where(row8 == SUBLANES - 1, pltpu.roll(next8, SUBLANES - 1, 0), rm1[tt - SUBLANES:tt])
    tail_out = (cb + w0 * r2[tt - SUBLANES:tt] + w1 * r1[tt - SUBLANES:tt]
                + w2 * main[tt - SUBLANES:tt] + w3 * tail_m1)
    return body, head_out, tail_out


def _gates_tile(rc_ref, a_ref, u_ref, wg_ref, bg_ref, sp, direction):
    hd = rc_ref.shape[1] // LRU_HEADS
    for h in range(LRU_HEADS):
        cols = slice(h * hd, (h + 1) * hd)
        xh = rc_ref[:, cols]
        pre = jnp.dot(xh.astype(BF16), wg_ref[direction, h], preferred_element_type=F32) + bg_ref[direction, h]
        r = jax.nn.sigmoid(pre[:, :hd])
        gi = jax.nn.sigmoid(pre[:, hd:])
        a = jnp.exp((-RG_C) * r * sp[:, cols])
        a_ref[:, cols] = a
        u_ref[:, cols] = jnp.sqrt(1.0 - a * a) * gi * xh


def _scan_tile(a_ref, u_ref, out_ref, carry_ref, reverse):
    tt, width = a_ref.shape
    groups = tt // SUBLANES
    row8 = lax.broadcasted_iota(jnp.int32, (SUBLANES, width), 0)

    def body(g, carry):
        gi = (groups - 1 - g) if reverse else g
        sl = pl.ds(pl.multiple_of(gi * SUBLANES, SUBLANES), SUBLANES)
        a = a_ref[sl, :]
        u = u_ref[sl, :]
        for k in (1, 2, 4):
            if reverse:
                shift = SUBLANES - k
                valid = row8 < SUBLANES - k
            else:
                shift = k
                valid = row8 >= k
            a_s = pltpu.roll(a, shift, 0)
            u_s = pltpu.roll(u, shift, 0)
            u = jnp.where(valid, a * u_s + u, u)
            a = jnp.where(valid, a * a_s, a)
        h = a * carry + u
        out_ref[0, sl, :] = h
        return h[0:1] if reverse else h[SUBLANES - 1:SUBLANES]

    carry_ref[...] = lax.fori_loop(0, groups, body, carry_ref[...])


def _scan_kernel(zf_ref, zfp_ref, zfn_ref, zb_ref, zbp_ref, zbn_ref, cw_ref, cb_ref, wg_ref, bg_ref,
                 lam_ref, hf_ref, hb_ref, rc_ref, a_ref, u_ref, cf_ref, cbk_ref, *, n_ctx_tiles, n_tiles):
    i = pl.program_id(1)
    tt = rc_ref.shape[0]

    @pl.when(i == 0)
    def _():
        cf_ref[...] = jnp.zeros_like(cf_ref)
        cbk_ref[...] = jnp.zeros_like(cbk_ref)

    jb = jnp.where(i < n_ctx_tiles, n_ctx_tiles - 1 - i, n_tiles - 1 - (i - n_ctx_tiles))
    cw = cw_ref[...]
    cb = cb_ref[...]
    sp = _softplus(-lam_ref[...])

    def run(z_ref, zp_ref, zn_ref, j, direction, out_ref, carry_ref):
        first = jnp.logical_or(j == 0, j == n_ctx_tiles)
        last = jnp.logical_or(j == n_ctx_tiles - 1, j == n_tiles - 1)
        prev8 = zp_ref[0] * jnp.where(first, 0.0, 1.0)
        next8 = zn_ref[0] * jnp.where(last, 0.0, 1.0)
        body, head_out, tail_out = _conv_tile(z_ref[0], prev8, next8, cw, cb)
        rc_ref[...] = body
        rc_ref[0:SUBLANES] = head_out
        rc_ref[tt - SUBLANES:tt] = tail_out
        _gates_tile(rc_ref, a_ref, u_ref, wg_ref, bg_ref, sp[direction:direction + 1], direction)
        _scan_tile(a_ref, u_ref, out_ref, carry_ref, reverse=(direction == 1))

    run(zf_ref, zfp_ref, zfn_ref, i, 0, hf_ref, cf_ref)
    run(zb_ref, zbp_ref, zbn_ref, jb, 1, hb_ref, cbk_ref)


def _scan(z, conv_w, conv_b, wg, bg, lam, n_ctx_tiles):
    bsz, s, _ = z.shape
    r = conv_w.shape[1]
    tt = TIME_TILE
    n_tiles = s // tt
    per = tt // SUBLANES
    last8 = s // SUBLANES - 1

    def bwd_tile(i):
        return jnp.where(i < n_ctx_tiles, n_ctx_tiles - 1 - i, n_tiles - 1 - (i - n_ctx_tiles))

    main_f = pl.BlockSpec((1, tt, r), lambda b, i: (b, i, 0))
    prev_f = pl.BlockSpec((1, SUBLANES, r), lambda b, i: (b, jnp.maximum(i * per - 1, 0), 0))
    next_f = pl.BlockSpec((1, SUBLANES, r), lambda b, i: (b, jnp.minimum((i + 1) * per, last8), 0))
    main_b = pl.BlockSpec((1, tt, r), lambda b, i: (b, bwd_tile(i), 0))
    prev_b = pl.BlockSpec((1, SUBLANES, r), lambda b, i: (b, jnp.maximum(bwd_tile(i) * per - 1, 0), 0))
    next_b = pl.BlockSpec((1, SUBLANES, r), lambda b, i: (b, jnp.minimum((bwd_tile(i) + 1) * per, last8), 0))
    full = lambda shape: pl.BlockSpec(shape, lambda b, i: (0,) * len(shape))
    return pl.pallas_call(
        functools.partial(_scan_kernel, n_ctx_tiles=n_ctx_tiles, n_tiles=n_tiles),
        grid=(bsz, n_tiles),
        in_specs=[main_f, prev_f, next_f, main_b, prev_b, next_b,
                  full(conv_w.shape), full(conv_b.shape), full(wg.shape), full(bg.shape), full(lam.shape)],
        out_specs=[pl.BlockSpec((1, tt, r), lambda b, i: (b, i, 0)),
                   pl.BlockSpec((1, tt, r), lambda b, i: (b, bwd_tile(i), 0))],
        out_shape=[jax.ShapeDtypeStruct((bsz, s, r), F32), jax.ShapeDtypeStruct((bsz, s, r), F32)],
        scratch_shapes=[pltpu.VMEM((tt, r), F32), pltpu.VMEM((tt, r), F32), pltpu.VMEM((tt, r), F32),
                        pltpu.VMEM((1, r), F32), pltpu.VMEM((1, r), F32)],
        compiler_params=pltpu.CompilerParams(
            dimension_semantics=("parallel", "arbitrary"), vmem_limit_bytes=VMEM_LIMIT),
        name="lru_scan",
    )(z, z, z, z, z, z, conv_w, conv_b, wg, bg, lam)


def _finish_kernel(zg_ref, zu_ref, zv_ref, hf_ref, hb_ref, x_ref, mod_ref, glru_ref, gmlp_ref, gv_ref,
                   ws_ref, bs_ref, wout_ref, wr_ref, br_ref,
                   xo_ref, tokp_ref, route_ref, wt_ref, cnt_ref, ym_ref, run_ref, *, n_experts):
    tt, r = ym_ref.shape
    hd = r // MLP_HEADS

    @pl.when(jnp.logical_and(pl.program_id(0) == 0, pl.program_id(1) == 0))
    def _():
        run_ref[...] = jnp.zeros_like(run_ref)

    y_lru = (hf_ref[0] + hb_ref[0]) * _gelu(zg_ref[0])
    yl = _rms(y_lru) * glru_ref[...]
    u = _gelu(zu_ref[0])
    vb = (_rms(_gelu(zv_ref[0])) * gv_ref[...]).astype(BF16)
    for ch in range(tt // CHUNK):
        rows = slice(ch * CHUNK, (ch + 1) * CHUNK)
        for h in range(MLP_HEADS):
            cols = slice(h * hd, (h + 1) * hd)
            mixed = jnp.dot(ws_ref[h], vb[rows, cols], preferred_element_type=F32) + bs_ref[h]
            ym_ref[rows, cols] = u[rows, cols] * mixed
    ym = _rms(ym_ref[...]) * gmlp_ref[...]
    y = (jnp.dot(yl.astype(BF16), wout_ref[0:r], preferred_element_type=F32)
         + jnp.dot(ym.astype(BF16), wout_ref[r:2 * r], preferred_element_type=F32))
    m = mod_ref[0, 0]
    xnew = x_ref[0] + m[2:3] * y
    xo_ref[0] = xnew
    tok = _rms(xnew) * (1.0 + m[4:5]) + m[3:4]

    half = tok.shape[1] // 2
    t_hi = tok.astype(BF16)
    t_hi32 = t_hi.astype(F32)
    bits = lax.bitcast_convert_type(t_hi32, U32)
    tokp_ref[0] = bits[:, :half] | (bits[:, half:] >> 16)

    t_lo = (tok - t_hi32).astype(BF16)
    logits = (jnp.dot(t_hi, wr_ref[0], preferred_element_type=F32)
              + jnp.dot(t_lo, wr_ref[0], preferred_element_type=F32)
              + jnp.dot(t_hi, wr_ref[1], preferred_element_type=F32)) + br_ref[...]
    lane = lax.broadcasted_iota(jnp.int32, logits.shape, 1)
    neg = jnp.float32(-jnp.inf)
    work = jnp.where(lane < n_experts, logits, neg)
    vals, idxs = [], []
    for _ in range(TOP_K):
        mx = jnp.max(work, axis=-1, keepdims=True)
        ix = jnp.min(jnp.where(work == mx, lane, LANES), axis=-1, keepdims=True)
        vals.append(mx)
        idxs.append(ix)
        work = jnp.where(lane == ix, neg, work)
    exps = [jnp.exp(v - vals[0]) for v in vals]
    denom = exps[0] + exps[1] + exps[2] + exps[3]

    hot = [lane == ix for ix in idxs]
    multi = jnp.zeros(logits.shape, F32)
    for k in range(TOP_K):
        multi = multi + hot[k].astype(F32)
    before = (lax.broadcasted_iota(jnp.int32, (tt, tt), 1) < lax.broadcasted_iota(jnp.int32, (tt, tt), 0))
    excl = jnp.dot(before.astype(BF16), multi.astype(BF16), preferred_element_type=F32) + run_ref[...]
    run_ref[...] = run_ref[...] + jnp.sum(multi, axis=0, keepdims=True)
    cnt_ref[...] = jnp.broadcast_to(run_ref[...], cnt_ref.shape).astype(jnp.int32)

    route = jnp.zeros(logits.shape, jnp.int32)
    wt_out = jnp.zeros(logits.shape, F32)
    for k in range(TOP_K):
        rank = jnp.sum(jnp.where(hot[k], excl, 0.0), axis=-1, keepdims=True).astype(jnp.int32)
        route = jnp.where(lane == k, idxs[k], route)
        route = jnp.where(lane == TOP_K + k, rank, route)
        wt_out = jnp.where(lane == k, exps[k] / denom, wt_out)
    route_ref[0] = route
    wt_ref[0] = wt_out


def _finish(z, hf, hb, x, modt, g_lru, g_mlp, g_v, ws, bs, w_out_bf16, wr_split, br_pad, n_experts,
            tile_offset, n_ctx_tiles):
    bsz, s, d = x.shape
    r = hf.shape[2]
    tt = TIME_TILE
    n_out_tiles = s // tt - tile_offset
    rows_out = n_out_tiles * tt
    off = tile_offset

    def zcol(c):
        return pl.BlockSpec((1, tt, r), lambda b, i: (b, i + off, c))

    full = lambda shape: pl.BlockSpec(shape, lambda b, i: (0,) * len(shape))
    tile_in = lambda w: pl.BlockSpec((1, tt, w), lambda b, i: (b, i + off, 0))
    tile_out = lambda w: pl.BlockSpec((1, tt, w), lambda b, i: (b, i, 0))
    return pl.pallas_call(
        functools.partial(_finish_kernel, n_experts=n_experts),
        grid=(bsz, n_out_tiles),
        in_specs=[zcol(1), zcol(2), zcol(3), tile_in(r), tile_in(r), tile_in(d),
                  pl.BlockSpec((1, 1, 6, d), lambda b, i: (b, jnp.where(i + off < n_ctx_tiles, 0, 1), 0, 0)),
                  full(g_lru.shape), full(g_mlp.shape), full(g_v.shape), full(ws.shape), full(bs.shape),
                  pl.BlockSpec(w_out_bf16.shape, lambda b, i: (0, 0), pipeline_mode=pl.Buffered(1)),
                  full(wr_split.shape), full(br_pad.shape)],
        out_specs=[tile_out(d), tile_out(d // 2), tile_out(LANES), tile_out(LANES),
                   pl.BlockSpec((SUBLANES, LANES), lambda b, i: (0, 0))],
        out_shape=[jax.ShapeDtypeStruct((bsz, rows_out, d), F32),
                   jax.ShapeDtypeStruct((bsz, rows_out, d // 2), U32),
                   jax.ShapeDtypeStruct((bsz, rows_out, LANES), jnp.int32),
                   jax.ShapeDtypeStruct((bsz, rows_out, LANES), F32),
                   jax.ShapeDtypeStruct((SUBLANES, LANES), jnp.int32)],
        scratch_shapes=[pltpu.VMEM((tt, r), F32), pltpu.VMEM((1, LANES), F32)],
        compiler_params=pltpu.CompilerParams(
            dimension_semantics=("arbitrary", "arbitrary"), vmem_limit_bytes=VMEM_LIMIT),
        name="mixer_finish",
    )(z, z, z, hf, hb, x, modt, g_lru, g_mlp, g_v, ws, bs, w_out_bf16, wr_split, br_pad)


def _row_copy(src_ref, src_row, dst_ref, dst_row, sem):
    return pltpu.make_async_copy(src_ref.at[pl.ds(src_row, 1)], dst_ref.at[pl.ds(dst_row, 1)], sem)


def _wait_rows(src_ref, dst_ref, sem, n_rows):
    def group(_, carry):
        for _ in range(WAIT_GROUP):
            _row_copy(src_ref, 0, dst_ref, 0, sem).wait()
        return carry

    lax.fori_loop(0, n_rows // WAIT_GROUP, group, 0)


def _zero_pieces(tile):
    return [1 << b for b in range(tile.bit_length() - 2, 2, -1)]


def _dispatch_kernel(starts_ref, zstart_ref, zlen_ref, idx_ref, rank_ref, tok_ref, xs_ref, zbuf, sem, zsem,
                     *, n_experts):
    s = pl.program_id(0)
    td = idx_ref.shape[2] // TOP_K
    pieces = _zero_pieces(EXPERT_TILE)

    @pl.when(s == 0)
    def _():
        zbuf[...] = jnp.zeros_like(zbuf)

        def fill(e, carry):
            gap = zlen_ref[e]
            start = zstart_ref[e]
            lead = jnp.minimum((SUBLANES - (start & (SUBLANES - 1))) & (SUBLANES - 1), gap)
            for i in range(SUBLANES - 1):
                @pl.when(i < lead)
                def _():
                    _row_copy(zbuf, 0, xs_ref, start + i, zsem).start()
            rest = gap - lead
            off = start + lead
            for p in pieces:
                @pl.when((rest & p) != 0)
                def _():
                    pltpu.make_async_copy(zbuf.at[pl.ds(0, p)],
                                          xs_ref.at[pl.ds(pl.multiple_of(off, SUBLANES), p)], zsem).start()

                off = off + (rest & p)
            for i in range(SUBLANES - 1):
                @pl.when(i < lead)
                def _():
                    _row_copy(zbuf, 0, xs_ref, 0, zsem).wait()
            for p in pieces:
                @pl.when((rest & p) != 0)
                def _():
                    pltpu.make_async_copy(zbuf.at[pl.ds(0, p)], xs_ref.at[pl.ds(0, p)], zsem).wait()
            return carry

        lax.fori_loop(0, n_experts, fill, 0)

        zrows = zbuf.shape[0]
        first_free = (zstart_ref[n_experts - 1] + zlen_ref[n_experts - 1]) // zrows
        n_chunks = xs_ref.shape[0] // zrows

        def fill_tail(c, carry):
            row = pl.multiple_of(c * zrows, zrows)
            cp = pltpu.make_async_copy(zbuf, xs_ref.at[pl.ds(row, zrows)], zsem)
            cp.start()
            cp.wait()
            return carry

        lax.fori_loop(first_free, n_chunks, fill_tail, 0)

    def issue(t, carry):
        for k in range(TOP_K):
            e = idx_ref[0, 0, t * TOP_K + k]
            d = starts_ref[e] + rank_ref[0, 0, t * TOP_K + k]
            _row_copy(tok_ref, t, xs_ref, d, sem).start()
        return carry

    lax.fori_loop(0, td, issue, 0, unroll=2)
    _wait_rows(tok_ref, xs_ref, sem, td * TOP_K)


def _dispatch(tokp, idx4, rank4, starts_p, zstart, zlen, n_rows):
    n_tok, dh = tokp.shape
    td = DISPATCH_TILE
    n_steps = n_tok // td
    n_experts = starts_p.shape[0]
    zrows = _zero_pieces(EXPERT_TILE)[0]
    return pl.pallas_call(
        functools.partial(_dispatch_kernel, n_experts=n_experts),
        grid_spec=pltpu.PrefetchScalarGridSpec(
            num_scalar_prefetch=3,
            grid=(n_steps,),
            in_specs=[pl.BlockSpec((1, 1, td * TOP_K), lambda s, a, b, c: (s, 0, 0), memory_space=pltpu.SMEM),
                      pl.BlockSpec((1, 1, td * TOP_K), lambda s, a, b, c: (s, 0, 0), memory_space=pltpu.SMEM),
                      pl.BlockSpec((td, dh), lambda s, a, b, c: (s, 0))],
            out_specs=pl.BlockSpec(memory_space=pl.ANY),
            scratch_shapes=[pltpu.VMEM((zrows, dh), U32), pltpu.SemaphoreType.DMA(()),
                            pltpu.SemaphoreType.DMA(())],
        ),
        out_shape=jax.ShapeDtypeStruct((n_rows, dh), U32),
        compiler_params=pltpu.CompilerParams(dimension_semantics=("arbitrary",)),
        name="expert_dispatch",
    )(starts_p, zstart, zlen, idx4.reshape(n_steps, 1, td * TOP_K), rank4.reshape(n_steps, 1, td * TOP_K), tokp)


def _expert_up_kernel(be_ref, nused_ref, x_ref, wg_ref, wu_ref, bgate_ref, bup_ref, h_ref, wg_s, wu_s):
    j = pl.program_id(1)
    changed = jnp.logical_or(j == 0, be_ref[j] != be_ref[jnp.maximum(j - 1, 0)])

    @pl.when(changed)
    def _():
        wg_s[...] = wg_ref[0, 0].astype(BF16)
        wu_s[...] = wu_ref[0, 0].astype(BF16)

    @pl.when(j < nused_ref[0])
    def _():
        xp = x_ref[...]
        half = xp.shape[1]
        xa = lax.bitcast_convert_type(xp & jnp.uint32(HI_MASK), F32).astype(BF16)
        xb = lax.bitcast_convert_type(xp << 16, F32).astype(BF16)
        tn = h_ref.shape[1]
        cw = tn // UP_COL_CHUNKS
        for c in range(UP_COL_CHUNKS):
            cols = slice(c * cw, (c + 1) * cw)
            gate = (jnp.dot(xa, wg_s[0:half, cols], preferred_element_type=F32)
                    + jnp.dot(xb, wg_s[half:2 * half, cols], preferred_element_type=F32)) + bgate_ref[0, 0, :, cols]
            up = (jnp.dot(xa, wu_s[0:half, cols], preferred_element_type=F32)
                  + jnp.dot(xb, wu_s[half:2 * half, cols], preferred_element_type=F32)) + bup_ref[0, 0, :, cols]
            gate = jnp.minimum(gate, SWIGLU_LIMIT)
            up = jnp.clip(up, -SWIGLU_LIMIT, SWIGLU_LIMIT)
            act = (up + 1.0) * gate * jax.nn.sigmoid(SWIGLU_ALPHA * gate)
            h_ref[:, cols] = act.astype(BF16)

    @pl.when(j >= nused_ref[0])
    def _():
        h_ref[...] = jnp.zeros_like(h_ref)


def _expert_up(xs, w_gu, b_gu, layer, block_e, n_used):
    n_rows, dh = xs.shape
    d = 2 * dh
    _, n_exp, _, two_f = w_gu.shape
    f = two_f // 2
    tm = EXPERT_TILE
    tn = UP_COLS
    n_blocks = n_rows // tm
    nf = f // tn
    b4 = b_gu.reshape(b_gu.shape[0], n_exp, 1, two_f)
    return pl.pallas_call(
        _expert_up_kernel,
        grid_spec=pltpu.PrefetchScalarGridSpec(
            num_scalar_prefetch=2,
            grid=(nf, n_blocks),
            in_specs=[pl.BlockSpec((tm, dh), lambda n, j, be, nu: (j, 0)),
                      pl.BlockSpec((1, 1, d, tn), lambda n, j, be, nu: (layer, be[j], 0, n)),
                      pl.BlockSpec((1, 1, d, tn), lambda n, j, be, nu: (layer, be[j], 0, nf + n)),
                      pl.BlockSpec((1, 1, 1, tn), lambda n, j, be, nu: (layer, be[j], 0, n)),
                      pl.BlockSpec((1, 1, 1, tn), lambda n, j, be, nu: (layer, be[j], 0, nf + n))],
            out_specs=pl.BlockSpec((tm, tn), lambda n, j, be, nu: (j, n)),
            scratch_shapes=[pltpu.VMEM((d, tn), BF16), pltpu.VMEM((d, tn), BF16)],
        ),
        out_shape=jax.ShapeDtypeStruct((n_rows, f), BF16),
        compiler_params=pltpu.CompilerParams(
            dimension_semantics=("arbitrary", "arbitrary"), vmem_limit_bytes=VMEM_LIMIT),
        name="expert_up",
    )(block_e, n_used, xs, w_gu, w_gu, b4, b4)


def _expert_down_kernel(be_ref, nused_ref, h_ref, wd_ref, bd_ref, o_ref, wd_s):
    j = pl.program_id(1)
    changed = jnp.logical_or(j == 0, be_ref[j] != be_ref[jnp.maximum(j - 1, 0)])

    @pl.when(changed)
    def _():
        wd_s[...] = wd_ref[0, 0].astype(BF16)

    @pl.when(j < nused_ref[0])
    def _():
        o_ref[...] = jnp.dot(h_ref[...], wd_s[...], preferred_element_type=F32) + bd_ref[0, 0]

    @pl.when(j >= nused_ref[0])
    def _():
        o_ref[...] = jnp.zeros_like(o_ref)


def _expert_down(h, w_down, b_down, layer, block_e, n_used):
    n_rows, f = h.shape
    _, n_exp, _, d = w_down.shape
    tm = EXPERT_TILE
    tn = 1024
    n_blocks = n_rows // tm
    return pl.pallas_call(
        _expert_down_kernel,
        grid_spec=pltpu.PrefetchScalarGridSpec(
            num_scalar_prefetch=2,
            grid=(d // tn, n_blocks),
            in_specs=[pl.BlockSpec((tm, f), lambda n, j, be, nu: (j, 0)),
                      pl.BlockSpec((1, 1, f, tn), lambda n, j, be, nu: (layer, be[j], 0, n)),
                      pl.BlockSpec((1, 1, 1, tn), lambda n, j, be, nu: (layer, be[j], 0, n))],
            out_specs=pl.BlockSpec((tm, tn), lambda n, j, be, nu: (j, n)),
            scratch_shapes=[pltpu.VMEM((f, tn), BF16)],
        ),
        out_shape=jax.ShapeDtypeStruct((n_rows, d), F32),
        compiler_params=pltpu.CompilerParams(
            dimension_semantics=("arbitrary", "arbitrary"), vmem_limit_bytes=VMEM_LIMIT),
        name="expert_down",
    )(block_e, n_used, h, w_down, b_down.reshape(b_down.shape[0], n_exp, 1, d))


def _combine_kernel(starts_ref, idxc_ref, rankc_ref, idxn_ref, rankn_ref, outs_ref, w_ref, x_ref, mod_ref,
                    gfin_ref, o_ref, buf, sem, *, final):
    t = pl.program_id(0)
    nt = pl.num_programs(0)
    tb = x_ref.shape[0]

    def issue(idx_ref, rank_ref, slot):
        def body(r, carry):
            for k in range(TOP_K):
                e = idx_ref[0, 0, r * TOP_K + k]
                p = starts_ref[e] + rank_ref[0, 0, r * TOP_K + k]
                pltpu.make_async_copy(outs_ref.at[pl.ds(p, 1)], buf.at[slot, k, pl.ds(r, 1)], sem.at[slot]).start()
            return carry

        lax.fori_loop(0, tb, body, 0, unroll=2)

    @pl.when(t == 0)
    def _():
        issue(idxc_ref, rankc_ref, 0)

    @pl.when(t + 1 < nt)
    def _():
        issue(idxn_ref, rankn_ref, (t + 1) % 2)

    slot = t % 2

    def group(_, carry):
        for _ in range(WAIT_GROUP):
            pltpu.make_async_copy(outs_ref.at[pl.ds(0, 1)], buf.at[slot, 0, pl.ds(0, 1)], sem.at[slot]).wait()
        return carry

    lax.fori_loop(0, tb * TOP_K // WAIT_GROUP, group, 0)

    w = w_ref[...]
    y = w[:, 0:1] * buf[slot, 0]
    for k in range(1, TOP_K):
        y = y + w[:, k:k + 1] * buf[slot, k]
    m = mod_ref[0, 0]
    xnew = x_ref[...] + m[5:6] * y
    if final:
        xnew = _rms(xnew) * gfin_ref[...]
    o_ref[...] = xnew


def _combine(outs, idx4, rank4, wt, starts_p, x_flat, modt, g_final, tiles_per_batch, n_ctx_tiles, final):
    n_tok, d = x_flat.shape
    tb = COMBINE_TILE
    n_tiles = n_tok // tb
    cur = lambda t, sp: (t, 0, 0)
    nxt = lambda t, sp: (jnp.minimum(t + 1, n_tiles - 1), 0, 0)
    smem = lambda imap: pl.BlockSpec((1, 1, tb * TOP_K), imap, memory_space=pltpu.SMEM)
    idx3 = idx4.reshape(n_tiles, 1, tb * TOP_K)
    rank3 = rank4.reshape(n_tiles, 1, tb * TOP_K)
    return pl.pallas_call(
        functools.partial(_combine_kernel, final=final),
        grid_spec=pltpu.PrefetchScalarGridSpec(
            num_scalar_prefetch=1,
            grid=(n_tiles,),
            in_specs=[smem(cur), smem(cur), smem(nxt), smem(nxt),
                      pl.BlockSpec(memory_space=pl.ANY),
                      pl.BlockSpec((tb, LANES), lambda t, sp: (t, 0)),
                      pl.BlockSpec((tb, d), lambda t, sp: (t, 0)),
                      pl.BlockSpec((1, 1, 6, d), lambda t, sp: (
                          t // tiles_per_batch, jnp.where(t % tiles_per_batch < n_ctx_tiles, 0, 1), 0, 0)),
                      pl.BlockSpec((1, d), lambda t, sp: (0, 0))],
            out_specs=pl.BlockSpec((tb, d), lambda t, sp: (t, 0)),
            scratch_shapes=[pltpu.VMEM((2, TOP_K, tb, d), F32), pltpu.SemaphoreType.DMA((2,))],
        ),
        out_shape=jax.ShapeDtypeStruct((n_tok, d), F32),
        compiler_params=pltpu.CompilerParams(
            dimension_semantics=("arbitrary",), vmem_limit_bytes=VMEM_LIMIT),
        name="expert_combine",
    )(starts_p, idx3, rank3, idx3, rank3, outs, wt, x_flat, modt, g_final.reshape(1, d))


def _expert_layout(counts, n_assign):
    n_experts = counts.shape[0]
    tm = EXPERT_TILE
    padded = (counts + tm - 1) // tm * tm
    ends_p = jnp.cumsum(padded).astype(jnp.int32)
    starts_p = ends_p - padded
    n_blocks = (n_assign + n_experts * (tm - 1) + tm - 1) // tm
    block_start = jnp.arange(n_blocks, dtype=jnp.int32) * tm
    block_e = jnp.minimum(jnp.sum((ends_p[None, :] <= block_start[:, None]).astype(jnp.int32), axis=1),
                          n_experts - 1).astype(jnp.int32)
    n_used = (ends_p[-1] // tm).astype(jnp.int32).reshape(1)
    return starts_p, starts_p + counts, padded - counts, block_e, n_used, n_blocks * tm


def _moe(tokp, route, counts, w_gu, b_gu, w_down, b_down, layer):
    n_tok = tokp.shape[0]
    idx4 = route[:, :TOP_K]
    rank4 = route[:, TOP_K:2 * TOP_K]
    starts_p, zstart, zlen, block_e, n_used, n_rows = _expert_layout(counts, n_tok * TOP_K)
    xs = _dispatch(tokp, idx4, rank4, starts_p, zstart, zlen, n_rows)
    h = _expert_up(xs, w_gu, b_gu, layer, block_e, n_used)
    outs = _expert_down(h, w_down, b_down, layer, block_e, n_used)
    return outs, idx4, rank4, starts_p


def kernel(x, c, ctx, c_ctx, w_mod, b_mod, w_in, conv_w, conv_b, w_r, b_r, w_i, b_i, lam, g_v, w_s, b_s,
           g_lru, g_mlp, w_out, w_router, b_router, w_gu, b_gu, w_down, b_down, g_final):
    bsz, n_lat, d = x.shape
    n_ctx = ctx.shape[1]
    depth = w_mod.shape[0]
    n_experts = w_router.shape[2]
    r = conv_w.shape[2]
    tt = TIME_TILE
    assert n_ctx % tt == 0 and n_lat % tt == 0 and tt % CHUNK == 0
    assert tt % DISPATCH_TILE == 0 and tt % COMBINE_TILE == 0
    assert n_experts <= LANES and r % LRU_HEADS == 0
    n_ctx_tiles = n_ctx // tt

    rows = (bsz + 1 + SUBLANES - 1) // SUBLANES * SUBLANES
    c_all = jnp.zeros((rows, d), F32).at[:bsz].set(c).at[bsz].set(c_ctx)
    mod = _modulation(c_all, w_mod, b_mod)

    hcat = jnp.concatenate([ctx, x], axis=1)
    out = None
    for l in range(depth):
        last = l == depth - 1
        mod_x = mod[l, :bsz].reshape(bsz, 1, 6, d)
        mod_c = jnp.broadcast_to(mod[l, bsz].reshape(1, 1, 6, d), (bsz, 1, 6, d))
        modt = jnp.concatenate([mod_c, mod_x], axis=1)

        z = _inproj(hcat, modt, w_in[l].astype(BF16), n_ctx_tiles)
        hd = r // LRU_HEADS
        wg = jnp.concatenate([w_r[l], w_i[l]], axis=-1).astype(BF16)
        bg = jnp.concatenate([b_r[l], b_i[l]], axis=-1).reshape(2, LRU_HEADS, 1, 2 * hd)
        hf, hb = _scan(z, conv_w[l], conv_b[l].reshape(1, r), wg, bg, lam[l], n_ctx_tiles)

        wr_hi = w_router[l].astype(BF16)
        wr_lo = (w_router[l] - wr_hi.astype(F32)).astype(BF16)
        wr_split = jnp.pad(jnp.stack([wr_hi, wr_lo]), ((0, 0), (0, 0), (0, LANES - n_experts)))
        br_pad = jnp.pad(b_router[l], (0, LANES - n_experts)).reshape(1, LANES)
        off = n_ctx_tiles if last else 0
        xo, tokp, route, wt, cnt = _finish(
            z, hf, hb, hcat, modt, g_lru[l].reshape(1, r), g_mlp[l].reshape(1, r), g_v[l].reshape(1, r),
            w_s[l].astype(BF16), b_s[l].reshape(MLP_HEADS, CHUNK, 1), w_out[l].astype(BF16),
            wr_split, br_pad, n_experts, off, n_ctx_tiles)

        rows_out = xo.shape[1]
        n_tok = bsz * rows_out
        outs, idx4, rank4, starts_p = _moe(
            tokp.reshape(n_tok, d // 2), route.reshape(n_tok, LANES), cnt[0, :n_experts],
            w_gu, b_gu, w_down, b_down, l)
        new = _combine(outs, idx4, rank4, wt.reshape(n_tok, LANES), starts_p, xo.reshape(n_tok, d), modt,
                       g_final, rows_out // COMBINE_TILE, 0 if last else n_ctx // COMBINE_TILE, last)
        new = new.reshape(bsz, rows_out, d)
        if last:
            out = new
        else:
            hcat = new
    return out
```

```python
import functools
import math

import jax
import jax.numpy as jnp
from jax import lax
from jax.experimental import pallas as pl
from jax.experimental.pallas import tpu as pltpu

LRU_HEADS = 8
MLP_HEADS = 8
CONV_W = 4
CONV_LEFT = 2
RG_C = 8.0
CHUNK = 128
TOP_K = 4
SWIGLU_LIMIT = 7.0
SWIGLU_ALPHA = 1.702
EPS = 1e-6

SUBLANES = 8
LANES = 128
TIME_TILE = 256
EXPERT_TILE = 512
DISPATCH_TILE = 128
COMBINE_TILE = 128
WAIT_GROUP = 64
UP_COLS = 1024
UP_COL_CHUNKS = 2
DOWN_COL_CHUNKS = 2
VMEM_LIMIT = 56 * 1024 * 1024

F32 = jnp.float32
BF16 = jnp.bfloat16
U32 = jnp.uint32
HI_MASK = 0xFFFF0000


def _rms(x):
    return x * lax.rsqrt(jnp.mean(x * x, axis=-1, keepdims=True) + EPS)


def _gelu(x):
    c = math.sqrt(2.0 / math.pi)
    return 0.5 * x * (1.0 + jnp.tanh(c * (x + 0.044715 * (x * x * x))))


def _softplus(x):
    return jnp.maximum(x, 0.0) + jnp.log1p(jnp.exp(-jnp.abs(x)))


def _mod_kernel(c_ref, w_ref, b_ref, o_ref):
    c = c_ref[...]
    s = c * jax.nn.sigmoid(c)
    o_ref[0] = jnp.dot(s, w_ref[0], preferred_element_type=F32) + b_ref[0]


def _modulation(c_all, w_mod, b_mod):
    depth, d, n6 = w_mod.shape
    rows = c_all.shape[0]
    tn = n6 // 8
    return pl.pallas_call(
        _mod_kernel,
        grid=(depth, n6 // tn),
        in_specs=[
            pl.BlockSpec((rows, d), lambda l, n: (0, 0)),
            pl.BlockSpec((1, d, tn), lambda l, n: (l, 0, n)),
            pl.BlockSpec((1, 1, tn), lambda l, n: (l, 0, n)),
        ],
        out_specs=pl.BlockSpec((1, rows, tn), lambda l, n: (l, 0, n)),
        out_shape=jax.ShapeDtypeStruct((depth, rows, n6), F32),
        compiler_params=pltpu.CompilerParams(
            dimension_semantics=("arbitrary", "arbitrary"), vmem_limit_bytes=VMEM_LIMIT),
        name="modulation",
    )(c_all, w_mod, b_mod.reshape(depth, 1, n6))


def _inproj_kernel(x_ref, mod_ref, w_ref, z_ref, *, n_chunks):
    x = x_ref[0]
    m = mod_ref[0, 0]
    nx = _rms(x) * (1.0 + m[1:2]) + m[0:1]
    nb = nx.astype(BF16)
    cw = w_ref.shape[1] // n_chunks
    for c in range(n_chunks):
        z_ref[0, :, c * cw:(c + 1) * cw] = jnp.dot(
            nb, w_ref[:, c * cw:(c + 1) * cw], preferred_element_type=F32)


def _inproj(x, modt, w_in_bf16, n_ctx_tiles):
    bsz, s, d = x.shape
    n_cols = w_in_bf16.shape[1]
    tt = TIME_TILE
    return pl.pallas_call(
        functools.partial(_inproj_kernel, n_chunks=4),
        grid=(bsz, s // tt),
        in_specs=[
            pl.BlockSpec((1, tt, d), lambda b, i: (b, i, 0)),
            pl.BlockSpec((1, 1, 6, d), lambda b, i: (b, jnp.where(i < n_ctx_tiles, 0, 1), 0, 0)),
            pl.BlockSpec((d, n_cols), lambda b, i: (0, 0), pipeline_mode=pl.Buffered(1)),
        ],
        out_specs=pl.BlockSpec((1, tt, n_cols), lambda b, i: (b, i, 0)),
        out_shape=jax.ShapeDtypeStruct((bsz, s, n_cols), F32),
        compiler_params=pltpu.CompilerParams(
            dimension_semantics=("parallel", "arbitrary"), vmem_limit_bytes=VMEM_LIMIT),
        name="inproj",
    )(x, modt, w_in_bf16)


def _conv_tile(main, prev8, next8, cw, cb):
    tt = main.shape[0]
    row8 = lax.broadcasted_iota(jnp.int32, (SUBLANES, main.shape[1]), 0)
    w0, w1, w2, w3 = cw[0:1], cw[1:2], cw[2:3], cw[3:4]
    r1 = pltpu.roll(main, 1, 0)
    r2 = pltpu.roll(main, 2, 0)
    rm1 = pltpu.roll(main, tt - 1, 0)
    body = cb + w0 * r2 + w1 * r1 + w2 * main + w3 * rm1
    head = main[0:SUBLANES]
    h1 = jnp.where(row8 < 1, pltpu.roll(prev8, 1, 0), pltpu.roll(head, 1, 0))
    h2 = jnp.where(row8 < 2, pltpu.roll(prev8, 2, 0), pltpu.roll(head, 2, 0))
    head_out = cb + w0 * h2 + w1 * h1 + w2 * head + w3 * rm1[0:SUBLANES]
    tail_m1 = jnp.where(row8 == SUBLANES - 1, pltpu.roll(next8, SUBLANES - 1, 0), rm1[tt - SUBLANES:tt])
    tail_out = (cb + w0 * r2[tt - SUBLANES:tt] + w1 * r1[tt - SUBLANES:tt]
                + w2 * main[tt - SUBLANES:tt] + w3 * tail_m1)
    return body, head_out, tail_out


def _gates_tile(rc_ref, a_ref, u_ref, wg_ref, bg_ref, sp, direction):
    hd = rc_ref.shape[1] // LRU_HEADS
    for h in range(LRU_HEADS):
        cols = slice(h * hd, (h + 1) * hd)
        xh = rc_ref[:, cols]
        pre = jnp.dot(xh.astype(BF16), wg_ref[direction, h], preferred_element_type=F32) + bg_ref[direction, h]
        r = jax.nn.sigmoid(pre[:, :hd])
        gi = jax.nn.sigmoid(pre[:, hd:])
        a = jnp.exp((-RG_C) * r * sp[:, cols])
        a_ref[:, cols] = a
        u_ref[:, cols] = jnp.sqrt(1.0 - a * a) * gi * xh


def _scan_tile(a_ref, u_ref, out_ref, carry_ref, reverse):
    tt, width = a_ref.shape
    groups = tt // SUBLANES
    row8 = lax.broadcasted_iota(jnp.int32, (SUBLANES, width), 0)

    def body(g, carry):
        gi = (groups - 1 - g) if reverse else g
        sl = pl.ds(pl.multiple_of(gi * SUBLANES, SUBLANES), SUBLANES)
        a = a_ref[sl, :]
        u = u_ref[sl, :]
        for k in (1, 2, 4):
            if reverse:
                shift = SUBLANES - k
                valid = row8 < SUBLANES - k
            else:
                shift = k
                valid = row8 >= k
            a_s = pltpu.roll(a, shift, 0)
            u_s = pltpu.roll(u, shift, 0)
            u = jnp.where(valid, a * u_s + u, u)
            a = jnp.where(valid, a * a_s, a)
        h = a * carry + u
        out_ref[0, sl, :] = h
        return h[0:1] if reverse else h[SUBLANES - 1:SUBLANES]

    carry_ref[...] = lax.fori_loop(0, groups, body, carry_ref[...])


def _scan_kernel(zf_ref, zfp_ref, zfn_ref, zb_ref, zbp_ref, zbn_ref, cw_ref, cb_ref, wg_ref, bg_ref,
                 lam_ref, hf_ref, hb_ref, rc_ref, a_ref, u_ref, cf_ref, cbk_ref, *, n_ctx_tiles, n_tiles):
    i = pl.program_id(1)
    tt = rc_ref.shape[0]

    @pl.when(i == 0)
    def _():
        cf_ref[...] = jnp.zeros_like(cf_ref)
        cbk_ref[...] = jnp.zeros_like(cbk_ref)

    jb = jnp.where(i < n_ctx_tiles, n_ctx_tiles - 1 - i, n_tiles - 1 - (i - n_ctx_tiles))
    cw = cw_ref[...]
    cb = cb_ref[...]
    sp = _softplus(-lam_ref[...])

    def run(z_ref, zp_ref, zn_ref, j, direction, out_ref, carry_ref):
        first = jnp.logical_or(j == 0, j == n_ctx_tiles)
        last = jnp.logical_or(j == n_ctx_tiles - 1, j == n_tiles - 1)
        prev8 = zp_ref[0] * jnp.where(first, 0.0, 1.0)
        next8 = zn_ref[0] * jnp.where(last, 0.0, 1.0)
        body, head_out, tail_out = _conv_tile(z_ref[0], prev8, next8, cw, cb)
        rc_ref[...] = body
        rc_ref[0:SUBLANES] = head_out
        rc_ref[tt - SUBLANES:tt] = tail_out
        _gates_tile(rc_ref, a_ref, u_ref, wg_ref, bg_ref, sp[direction:direction + 1], direction)
        _scan_tile(a_ref, u_ref, out_ref, carry_ref, reverse=(direction == 1))

    run(zf_ref, zfp_ref, zfn_ref, i, 0, hf_ref, cf_ref)
    run(zb_ref, zbp_ref, zbn_ref, jb, 1, hb_ref, cbk_ref)


def _scan(z, conv_w, conv_b, wg, bg, lam, n_ctx_tiles):
    bsz, s, _ = z.shape
    r = conv_w.shape[1]
    tt = TIME_TILE
    n_tiles = s // tt
    per = tt // SUBLANES
    last8 = s // SUBLANES - 1

    def bwd_tile(i):
        return jnp.where(i < n_ctx_tiles, n_ctx_tiles - 1 - i, n_tiles - 1 - (i - n_ctx_tiles))

    main_f = pl.BlockSpec((1, tt, r), lambda b, i: (b, i, 0))
    prev_f = pl.BlockSpec((1, SUBLANES, r), lambda b, i: (b, jnp.maximum(i * per - 1, 0), 0))
    next_f = pl.BlockSpec((1, SUBLANES, r), lambda b, i: (b, jnp.minimum((i + 1) * per, last8), 0))
    main_b = pl.BlockSpec((1, tt, r), lambda b, i: (b, bwd_tile(i), 0))
    prev_b = pl.BlockSpec((1, SUBLANES, r), lambda b, i: (b, jnp.maximum(bwd_tile(i) * per - 1, 0), 0))
    next_b = pl.BlockSpec((1, SUBLANES, r), lambda b, i: (b, jnp.minimum((bwd_tile(i) + 1) * per, last8), 0))
    full = lambda shape: pl.BlockSpec(shape, lambda b, i: (0,) * len(shape))
    return pl.pallas_call(
        functools.partial(_scan_kernel, n_ctx_tiles=n_ctx_tiles, n_tiles=n_tiles),
        grid=(bsz, n_tiles),
        in_specs=[main_f, prev_f, next_f, main_b, prev_b, next_b,
                  full(conv_w.shape), full(conv_b.shape), full(wg.shape), full(bg.shape), full(lam.shape)],
        out_specs=[pl.BlockSpec((1, tt, r), lambda b, i: (b, i, 0)),
                   pl.BlockSpec((1, tt, r), lambda b, i: (b, bwd_tile(i), 0))],
        out_shape=[jax.ShapeDtypeStruct((bsz, s, r), F32), jax.ShapeDtypeStruct((bsz, s, r), F32)],
        scratch_shapes=[pltpu.VMEM((tt, r), F32), pltpu.VMEM((tt, r), F32), pltpu.VMEM((tt, r), F32),
                        pltpu.VMEM((1, r), F32), pltpu.VMEM((1, r), F32)],
        compiler_params=pltpu.CompilerParams(
            dimension_semantics=("parallel", "arbitrary"), vmem_limit_bytes=VMEM_LIMIT),
        name="lru_scan",
    )(z, z, z, z, z, z, conv_w, conv_b, wg, bg, lam)


def _finish_kernel(zg_ref, zu_ref, zv_ref, hf_ref, hb_ref, x_ref, mod_ref, glru_ref, gmlp_ref, gv_ref,
                   ws_ref, bs_ref, wout_ref, wr_ref, br_ref,
                   xo_ref, tokp_ref, route_ref, wt_ref, cnt_ref, ym_ref, run_ref, *, n_experts):
    tt, r = ym_ref.shape
    hd = r // MLP_HEADS

    @pl.when(jnp.logical_and(pl.program_id(0) == 0, pl.program_id(1) == 0))
    def _():
        run_ref[...] = jnp.zeros_like(run_ref)

    y_lru = (hf_ref[0] + hb_ref[0]) * _gelu(zg_ref[0])
    yl = _rms(y_lru) * glru_ref[...]
    u = _gelu(zu_ref[0])
    vb = (_rms(_gelu(zv_ref[0])) * gv_ref[...]).astype(BF16)
    for ch in range(tt // CHUNK):
        rows = slice(ch * CHUNK, (ch + 1) * CHUNK)
        for h in range(MLP_HEADS):
            cols = slice(h * hd, (h + 1) * hd)
            mixed = jnp.dot(ws_ref[h], vb[rows, cols], preferred_element_type=F32) + bs_ref[h]
            ym_ref[rows, cols] = u[rows, cols] * mixed
    ym = _rms(ym_ref[...]) * gmlp_ref[...]
    y = (jnp.dot(yl.astype(BF16), wout_ref[0:r], preferred_element_type=F32)
         + jnp.dot(ym.astype(BF16), wout_ref[r:2 * r], preferred_element_type=F32))
    m = mod_ref[0, 0]
    xnew = x_ref[0] + m[2:3] * y
    xo_ref[0] = xnew
    tok = _rms(xnew) * (1.0 + m[4:5]) + m[3:4]

    half = tok.shape[1] // 2
    t_hi = tok.astype(BF16)
    t_hi32 = t_hi.astype(F32)
    bits = lax.bitcast_convert_type(t_hi32, U32)
    tokp_ref[0] = bits[:, :half] | (bits[:, half:] >> 16)

    t_lo = (tok - t_hi32).astype(BF16)
    logits = (jnp.dot(t_hi, wr_ref[0], preferred_element_type=F32)
              + jnp.dot(t_lo, wr_ref[0], preferred_element_type=F32)
              + jnp.dot(t_hi, wr_ref[1], preferred_element_type=F32)) + br_ref[...]
    lane = lax.broadcasted_iota(jnp.int32, logits.shape, 1)
    neg = jnp.float32(-jnp.inf)
    work = jnp.where(lane < n_experts, logits, neg)
    vals, idxs = [], []
    for _ in range(TOP_K):
        mx = jnp.max(work, axis=-1, keepdims=True)
        ix = jnp.min(jnp.where(work == mx, lane, LANES), axis=-1, keepdims=True)
        vals.append(mx)
        idxs.append(ix)
        work = jnp.where(lane == ix, neg, work)
    exps = [jnp.exp(v - vals[0]) for v in vals]
    denom = exps[0] + exps[1] + exps[2] + exps[3]

    hot = [lane == ix for ix in idxs]
    multi = jnp.zeros(logits.shape, F32)
    for k in range(TOP_K):
        multi = multi + hot[k].astype(F32)
    before = (lax.broadcasted_iota(jnp.int32, (tt, tt), 1) < lax.broadcasted_iota(jnp.int32, (tt, tt), 0))
    excl = jnp.dot(before.astype(BF16), multi.astype(BF16), preferred_element_type=F32) + run_ref[...]
    run_ref[...] = run_ref[...] + jnp.sum(multi, axis=0, keepdims=True)
    cnt_ref[...] = jnp.broadcast_to(run_ref[...], cnt_ref.shape).astype(jnp.int32)

    route = jnp.zeros(logits.shape, jnp.int32)
    wt_out = jnp.zeros(logits.shape, F32)
    for k in range(TOP_K):
        rank = jnp.sum(jnp.where(hot[k], excl, 0.0), axis=-1, keepdims=True).astype(jnp.int32)
        route = jnp.where(lane == k, idxs[k], route)
        route = jnp.where(lane == TOP_K + k, rank, route)
        wt_out = jnp.where(lane == k, exps[k] / denom, wt_out)
    route_ref[0] = route
    wt_ref[0] = wt_out


def _finish(z, hf, hb, x, modt, g_lru, g_mlp, g_v, ws, bs, w_out_bf16, wr_split, br_pad, n_experts,
            tile_offset, n_ctx_tiles):
    bsz, s, d = x.shape
    r = hf.shape[2]
    tt = TIME_TILE
    n_out_tiles = s // tt - tile_offset
    rows_out = n_out_tiles * tt
    off = tile_offset

    def zcol(c):
        return pl.BlockSpec((1, tt, r), lambda b, i: (b, i + off, c))

    full = lambda shape: pl.BlockSpec(shape, lambda b, i: (0,) * len(shape))
    tile_in = lambda w: pl.BlockSpec((1, tt, w), lambda b, i: (b, i + off, 0))
    tile_out = lambda w: pl.BlockSpec((1, tt, w), lambda b, i: (b, i, 0))
    return pl.pallas_call(
        functools.partial(_finish_kernel, n_experts=n_experts),
        grid=(bsz, n_out_tiles),
        in_specs=[zcol(1), zcol(2), zcol(3), tile_in(r), tile_in(r), tile_in(d),
                  pl.BlockSpec((1, 1, 6, d), lambda b, i: (b, jnp.where(i + off < n_ctx_tiles, 0, 1), 0, 0)),
                  full(g_lru.shape), full(g_mlp.shape), full(g_v.shape), full(ws.shape), full(bs.shape),
                  pl.BlockSpec(w_out_bf16.shape, lambda b, i: (0, 0), pipeline_mode=pl.Buffered(1)),
                  full(wr_split.shape), full(br_pad.shape)],
        out_specs=[tile_out(d), tile_out(d // 2), tile_out(LANES), tile_out(LANES),
                   pl.BlockSpec((SUBLANES, LANES), lambda b, i: (0, 0))],
        out_shape=[jax.ShapeDtypeStruct((bsz, rows_out, d), F32),
                   jax.ShapeDtypeStruct((bsz, rows_out, d // 2), U32),
                   jax.ShapeDtypeStruct((bsz, rows_out, LANES), jnp.int32),
                   jax.ShapeDtypeStruct((bsz, rows_out, LANES), F32),
                   jax.ShapeDtypeStruct((SUBLANES, LANES), jnp.int32)],
        scratch_shapes=[pltpu.VMEM((tt, r), F32), pltpu.VMEM((1, LANES), F32)],
        compiler_params=pltpu.CompilerParams(
            dimension_semantics=("arbitrary", "arbitrary"), vmem_limit_bytes=VMEM_LIMIT),
        name="mixer_finish",
    )(z, z, z, hf, hb, x, modt, g_lru, g_mlp, g_v, ws, bs, w_out_bf16, wr_split, br_pad)


def _row_copy(src_ref, src_row, dst_ref, dst_row, sem):
    return pltpu.make_async_copy(src_ref.at[pl.ds(src_row, 1)], dst_ref.at[pl.ds(dst_row, 1)], sem)


def _wait_rows(src_ref, dst_ref, sem, n_rows):
    def group(_, carry):
        for _ in range(WAIT_GROUP):
            _row_copy(src_ref, 0, dst_ref, 0, sem).wait()
        return carry

    lax.fori_loop(0, n_rows // WAIT_GROUP, group, 0)


def _zero_pieces(tile):
    return [1 << b for b in range(tile.bit_length() - 2, 2, -1)]


def _dispatch_kernel(zstart_ref, zlen_ref, dest_ref, tok_ref, xs_ref, zbuf, sem, zsem, *, n_experts):
    s = pl.program_id(0)
    td = dest_ref.shape[2] // TOP_K
    pieces = _zero_pieces(EXPERT_TILE)

    @pl.when(s == 0)
    def _():
        zbuf[...] = jnp.zeros_like(zbuf)

        def fill(e, carry):
            gap = zlen_ref[e]
            start = zstart_ref[e]
            lead = jnp.minimum((SUBLANES - (start & (SUBLANES - 1))) & (SUBLANES - 1), gap)
            for i in range(SUBLANES - 1):
                @pl.when(i < lead)
                def _():
                    _row_copy(zbuf, 0, xs_ref, start + i, zsem).start()
            rest = gap - lead
            off = start + lead
            for p in pieces:
                @pl.when((rest & p) != 0)
                def _():
                    pltpu.make_async_copy(zbuf.at[pl.ds(0, p)],
                                          xs_ref.at[pl.ds(pl.multiple_of(off, SUBLANES), p)], zsem).start()

                off = off + (rest & p)
            for i in range(SUBLANES - 1):
                @pl.when(i < lead)
                def _():
                    _row_copy(zbuf, 0, xs_ref, 0, zsem).wait()
            for p in pieces:
                @pl.when((rest & p) != 0)
                def _():
                    pltpu.make_async_copy(zbuf.at[pl.ds(0, p)], xs_ref.at[pl.ds(0, p)], zsem).wait()
            return carry

        lax.fori_loop(0, n_experts, fill, 0)

        zrows = zbuf.shape[0]
        first_free = (zstart_ref[n_experts - 1] + zlen_ref[n_experts - 1]) // zrows
        n_chunks = xs_ref.shape[0] // zrows

        def fill_tail(c, carry):
            row = pl.multiple_of(c * zrows, zrows)
            cp = pltpu.make_async_copy(zbuf, xs_ref.at[pl.ds(row, zrows)], zsem)
            cp.start()
            cp.wait()
            return carry

        lax.fori_loop(first_free, n_chunks, fill_tail, 0)

    def issue(t, carry):
        for k in range(TOP_K):
            _row_copy(tok_ref, t, xs_ref, dest_ref[0, 0, t * TOP_K + k], sem).start()
        return carry

    lax.fori_loop(0, td, issue, 0, unroll=4)
    _wait_rows(tok_ref, xs_ref, sem, td * TOP_K)


def _dispatch(tokp, dest, zstart, zlen, n_rows):
    n_tok, dh = tokp.shape
    td = DISPATCH_TILE
    n_steps = n_tok // td
    n_experts = zstart.shape[0]
    zrows = _zero_pieces(EXPERT_TILE)[0]
    return pl.pallas_call(
        functools.partial(_dispatch_kernel, n_experts=n_experts),
        grid_spec=pltpu.PrefetchScalarGridSpec(
            num_scalar_prefetch=2,
            grid=(n_steps,),
            in_specs=[pl.BlockSpec((1, 1, td * TOP_K), lambda s, a, b: (s, 0, 0), memory_space=pltpu.SMEM),
                      pl.BlockSpec((td, dh), lambda s, a, b: (s, 0))],
            out_specs=pl.BlockSpec(memory_space=pl.ANY),
            scratch_shapes=[pltpu.VMEM((zrows, dh), U32), pltpu.SemaphoreType.DMA(()),
                            pltpu.SemaphoreType.DMA(())],
        ),
        out_shape=jax.ShapeDtypeStruct((n_rows, dh), U32),
        compiler_params=pltpu.CompilerParams(dimension_semantics=("arbitrary",)),
        name="expert_dispatch",
    )(zstart, zlen, dest.reshape(n_steps, 1, td * TOP_K), tokp)


def _expert_up_kernel(be_ref, nused_ref, x_ref, wg_ref, wu_ref, bgate_ref, bup_ref, h_ref, wg_s, wu_s):
    j = pl.program_id(1)
    changed = jnp.logical_or(j == 0, be_ref[j] != be_ref[jnp.maximum(j - 1, 0)])

    @pl.when(changed)
    def _():
        wg_s[...] = wg_ref[0, 0].astype(BF16)
        wu_s[...] = wu_ref[0, 0].astype(BF16)

    @pl.when(j < nused_ref[0])
    def _():
        xp = x_ref[...]
        half = xp.shape[1]
        xa = lax.bitcast_convert_type(xp & jnp.uint32(HI_MASK), F32).astype(BF16)
        xb = lax.bitcast_convert_type(xp << 16, F32).astype(BF16)
        tn = h_ref.shape[1]
        cw = tn // UP_COL_CHUNKS
        for c in range(UP_COL_CHUNKS):
            cols = slice(c * cw, (c + 1) * cw)
            gate = (jnp.dot(xa, wg_s[0:half, cols], preferred_element_type=F32)
                    + jnp.dot(xb, wg_s[half:2 * half, cols], preferred_element_type=F32)) + bgate_ref[0, 0, :, cols]
            up = (jnp.dot(xa, wu_s[0:half, cols], preferred_element_type=F32)
                  + jnp.dot(xb, wu_s[half:2 * half, cols], preferred_element_type=F32)) + bup_ref[0, 0, :, cols]
            gate = jnp.minimum(gate, SWIGLU_LIMIT)
            up = jnp.clip(up, -SWIGLU_LIMIT, SWIGLU_LIMIT)
            act = (up + 1.0) * gate * jax.nn.sigmoid(SWIGLU_ALPHA * gate)
            h_ref[:, cols] = act.astype(BF16)

    @pl.when(j >= nused_ref[0])
    def _():
        h_ref[...] = jnp.zeros_like(h_ref)


def _expert_up(xs, w_gu, b_gu, layer, block_e, n_used):
    n_rows, dh = xs.shape
    d = 2 * dh
    _, n_exp, _, two_f = w_gu.shape
    f = two_f // 2
    tm = EXPERT_TILE
    tn = UP_COLS
    n_blocks = n_rows // tm
    nf = f // tn
    b4 = b_gu.reshape(b_gu.shape[0], n_exp, 1, two_f)
    return pl.pallas_call(
        _expert_up_kernel,
        grid_spec=pltpu.PrefetchScalarGridSpec(
            num_scalar_prefetch=2,
            grid=(nf, n_blocks),
            in_specs=[pl.BlockSpec((tm, dh), lambda n, j, be, nu: (j, 0)),
                      pl.BlockSpec((1, 1, d, tn), lambda n, j, be, nu: (layer, be[j], 0, n)),
                      pl.BlockSpec((1, 1, d, tn), lambda n, j, be, nu: (layer, be[j], 0, nf + n)),
                      pl.BlockSpec((1, 1, 1, tn), lambda n, j, be, nu: (layer, be[j], 0, n)),
                      pl.BlockSpec((1, 1, 1, tn), lambda n, j, be, nu: (layer, be[j], 0, nf + n))],
            out_specs=pl.BlockSpec((tm, tn), lambda n, j, be, nu: (j, n)),
            scratch_shapes=[pltpu.VMEM((d, tn), BF16), pltpu.VMEM((d, tn), BF16)],
        ),
        out_shape=jax.ShapeDtypeStruct((n_rows, f), BF16),
        compiler_params=pltpu.CompilerParams(
            dimension_semantics=("arbitrary", "arbitrary"), vmem_limit_bytes=VMEM_LIMIT),
        name="expert_up",
    )(block_e, n_used, xs, w_gu, w_gu, b4, b4)


def _expert_down_kernel(be_ref, nused_ref, h_ref, wd_ref, bd_ref, o_ref, wd_s):
    j = pl.program_id(1)
    changed = jnp.logical_or(j == 0, be_ref[j] != be_ref[jnp.maximum(j - 1, 0)])

    @pl.when(changed)
    def _():
        wd_s[...] = wd_ref[0, 0].astype(BF16)

    @pl.when(j < nused_ref[0])
    def _():
        hb = h_ref[...]
        half = o_ref.shape[1]
        cw = half // DOWN_COL_CHUNKS
        for c in range(DOWN_COL_CHUNKS):
            lo = slice(c * cw, (c + 1) * cw)
            hi = slice(half + c * cw, half + (c + 1) * cw)
            ya = jnp.dot(hb, wd_s[:, lo], preferred_element_type=F32) + bd_ref[0, 0, :, lo]
            yb = jnp.dot(hb, wd_s[:, hi], preferred_element_type=F32) + bd_ref[0, 0, :, hi]
            ba = lax.bitcast_convert_type(ya.astype(BF16).astype(F32), U32)
            bb = lax.bitcast_convert_type(yb.astype(BF16).astype(F32), U32)
            o_ref[:, lo] = ba | (bb >> 16)

    @pl.when(j >= nused_ref[0])
    def _():
        o_ref[...] = jnp.zeros_like(o_ref)


def _expert_down(h, w_down, b_down, layer, block_e, n_used):
    n_rows, f = h.shape
    _, n_exp, _, d = w_down.shape
    tm = EXPERT_TILE
    n_blocks = n_rows // tm
    return pl.pallas_call(
        _expert_down_kernel,
        grid_spec=pltpu.PrefetchScalarGridSpec(
            num_scalar_prefetch=2,
            grid=(1, n_blocks),
            in_specs=[pl.BlockSpec((tm, f), lambda n, j, be, nu: (j, 0)),
                      pl.BlockSpec((1, 1, f, d), lambda n, j, be, nu: (layer, be[j], 0, 0)),
                      pl.BlockSpec((1, 1, 1, d), lambda n, j, be, nu: (layer, be[j], 0, 0))],
            out_specs=pl.BlockSpec((tm, d // 2), lambda n, j, be, nu: (j, 0)),
            scratch_shapes=[pltpu.VMEM((f, d), BF16)],
        ),
        out_shape=jax.ShapeDtypeStruct((n_rows, d // 2), U32),
        compiler_params=pltpu.CompilerParams(
            dimension_semantics=("arbitrary", "arbitrary"), vmem_limit_bytes=VMEM_LIMIT),
        name="expert_down",
    )(block_e, n_used, h, w_down, b_down.reshape(b_down.shape[0], n_exp, 1, d))


def _combine_kernel(destc_ref, destn_ref, outs_ref, w_ref, x_ref, mod_ref, gfin_ref, o_ref, buf, sem, *, final):
    t = pl.program_id(0)
    nt = pl.num_programs(0)
    tb = x_ref.shape[0]

    def issue(dest_ref, slot):
        def body(r, carry):
            for k in range(TOP_K):
                p = dest_ref[0, 0, r * TOP_K + k]
                pltpu.make_async_copy(outs_ref.at[pl.ds(p, 1)], buf.at[slot, k, pl.ds(r, 1)], sem.at[slot]).start()
            return carry

        lax.fori_loop(0, tb, body, 0, unroll=4)

    @pl.when(t == 0)
    def _():
        issue(destc_ref, 0)

    @pl.when(t + 1 < nt)
    def _():
        issue(destn_ref, (t + 1) % 2)

    slot = t % 2

    def group(_, carry):
        for _ in range(WAIT_GROUP):
            pltpu.make_async_copy(outs_ref.at[pl.ds(0, 1)], buf.at[slot, 0, pl.ds(0, 1)], sem.at[slot]).wait()
        return carry

    lax.fori_loop(0, tb * TOP_K // WAIT_GROUP, group, 0)

    w = w_ref[...]
    half = buf.shape[3]
    ya = jnp.zeros((tb, half), F32)
    yb = jnp.zeros((tb, half), F32)
    for k in range(TOP_K):
        p = buf[slot, k]
        wk = w[:, k:k + 1]
        ya = ya + wk * lax.bitcast_convert_type(p & jnp.uint32(HI_MASK), F32)
        yb = yb + wk * lax.bitcast_convert_type(p << 16, F32)
    m = mod_ref[0, 0]
    xa = x_ref[:, 0:half] + m[5:6, 0:half] * ya
    xb = x_ref[:, half:2 * half] + m[5:6, half:2 * half] * yb
    if final:
        ms = (jnp.sum(xa * xa, axis=-1, keepdims=True) + jnp.sum(xb * xb, axis=-1, keepdims=True)) / (2 * half)
        scale = lax.rsqrt(ms + EPS)
        xa = xa * scale * gfin_ref[:, 0:half]
        xb = xb * scale * gfin_ref[:, half:2 * half]
    o_ref[:, 0:half] = xa
    o_ref[:, half:2 * half] = xb


def _combine(outs, dest, wt, x_flat, modt, g_final, tiles_per_batch, n_ctx_tiles, final):
    n_tok, d = x_flat.shape
    tb = COMBINE_TILE
    n_tiles = n_tok // tb
    cur = lambda t: (t, 0, 0)
    nxt = lambda t: (jnp.minimum(t + 1, n_tiles - 1), 0, 0)
    smem = lambda imap: pl.BlockSpec((1, 1, tb * TOP_K), imap, memory_space=pltpu.SMEM)
    dest3 = dest.reshape(n_tiles, 1, tb * TOP_K)
    return pl.pallas_call(
        functools.partial(_combine_kernel, final=final),
        grid=(n_tiles,),
        in_specs=[smem(cur), smem(nxt),
                  pl.BlockSpec(memory_space=pl.ANY),
                  pl.BlockSpec((tb, LANES), lambda t: (t, 0)),
                  pl.BlockSpec((tb, d), lambda t: (t, 0)),
                  pl.BlockSpec((1, 1, 6, d), lambda t: (
                      t // tiles_per_batch, jnp.where(t % tiles_per_batch < n_ctx_tiles, 0, 1), 0, 0)),
                  pl.BlockSpec((1, d), lambda t: (0, 0))],
        out_specs=pl.BlockSpec((tb, d), lambda t: (t, 0)),
        out_shape=jax.ShapeDtypeStruct((n_tok, d), F32),
        scratch_shapes=[pltpu.VMEM((2, TOP_K, tb, d // 2), U32), pltpu.SemaphoreType.DMA((2,))],
        compiler_params=pltpu.CompilerParams(
            dimension_semantics=("arbitrary",), vmem_limit_bytes=VMEM_LIMIT),
        name="expert_combine",
    )(dest3, dest3, outs, wt, x_flat, modt, g_final.reshape(1, d))


def _expert_layout(counts, n_assign):
    n_experts = counts.shape[0]
    tm = EXPERT_TILE
    padded = (counts + tm - 1) // tm * tm
    ends_p = jnp.cumsum(padded).astype(jnp.int32)
    starts_p = ends_p - padded
    n_blocks = (n_assign + n_experts * (tm - 1) + tm - 1) // tm
    block_start = jnp.arange(n_blocks, dtype=jnp.int32) * tm
    block_e = jnp.minimum(jnp.sum((ends_p[None, :] <= block_start[:, None]).astype(jnp.int32), axis=1),
                          n_experts - 1).astype(jnp.int32)
    n_used = (ends_p[-1] // tm).astype(jnp.int32).reshape(1)
    return starts_p, starts_p + counts, padded - counts, block_e, n_used, n_blocks * tm


def _moe(tokp, route, counts, w_gu, b_gu, w_down, b_down, layer):
    n_tok = tokp.shape[0]
    n_experts = counts.shape[0]
    idx4 = route[:, :TOP_K]
    rank4 = route[:, TOP_K:2 * TOP_K]
    starts_p, zstart, zlen, block_e, n_used, n_rows = _expert_layout(counts, n_tok * TOP_K)
    onehot = idx4[:, :, None] == jnp.arange(n_experts, dtype=jnp.int32)[None, None, :]
    dest = (jnp.sum(jnp.where(onehot, starts_p[None, None, :], 0), axis=-1) + rank4).astype(jnp.int32)
    xs = _dispatch(tokp, dest, zstart, zlen, n_rows)
    h = _expert_up(xs, w_gu, b_gu, layer, block_e, n_used)
    outs = _expert_down(h, w_down, b_down, layer, block_e, n_used)
    return outs, dest


def kernel(x, c, ctx, c_ctx, w_mod, b_mod, w_in, conv_w, conv_b, w_r, b_r, w_i, b_i, lam, g_v, w_s, b_s,
           g_lru, g_mlp, w_out, w_router, b_router, w_gu, b_gu, w_down, b_down, g_final):
    bsz, n_lat, d = x.shape
    n_ctx = ctx.shape[1]
    depth = w_mod.shape[0]
    n_experts = w_router.shape[2]
    r = conv_w.shape[2]
    tt = TIME_TILE
    assert n_ctx % tt == 0 and n_lat % tt == 0 and tt % CHUNK == 0
    assert tt % DISPATCH_TILE == 0 and tt % COMBINE_TILE == 0
    assert n_experts <= LANES and r % LRU_HEADS == 0
    n_ctx_tiles = n_ctx // tt

    rows = (bsz + 1 + SUBLANES - 1) // SUBLANES * SUBLANES
    c_all = jnp.zeros((rows, d), F32).at[:bsz].set(c).at[bsz].set(c_ctx)
    mod = _modulation(c_all, w_mod, b_mod)

    hcat = jnp.concatenate([ctx, x], axis=1)
    out = None
    for l in range(depth):
        last = l == depth - 1
        mod_x = mod[l, :bsz].reshape(bsz, 1, 6, d)
        mod_c = jnp.broadcast_to(mod[l, bsz].reshape(1, 1, 6, d), (bsz, 1, 6, d))
        modt = jnp.concatenate([mod_c, mod_x], axis=1)

        z = _inproj(hcat, modt, w_in[l].astype(BF16), n_ctx_tiles)
        hd = r // LRU_HEADS
        wg = jnp.concatenate([w_r[l], w_i[l]], axis=-1).astype(BF16)
        bg = jnp.concatenate([b_r[l], b_i[l]], axis=-1).reshape(2, LRU_HEADS, 1, 2 * hd)
        hf, hb = _scan(z, conv_w[l], conv_b[l].reshape(1, r), wg, bg, lam[l], n_ctx_tiles)

        wr_hi = w_router[l].astype(BF16)
        wr_lo = (w_router[l] - wr_hi.astype(F32)).astype(BF16)
        wr_split = jnp.pad(jnp.stack([wr_hi, wr_lo]), ((0, 0), (0, 0), (0, LANES - n_experts)))
        br_pad = jnp.pad(b_router[l], (0, LANES - n_experts)).reshape(1, LANES)
        off = n_ctx_tiles if last else 0
        xo, tokp, route, wt, cnt = _finish(
            z, hf, hb, hcat, modt, g_lru[l].reshape(1, r), g_mlp[l].reshape(1, r), g_v[l].reshape(1, r),
            w_s[l].astype(BF16), b_s[l].reshape(MLP_HEADS, CHUNK, 1), w_out[l].astype(BF16),
            wr_split, br_pad, n_experts, off, n_ctx_tiles)

        rows_out = xo.shape[1]
        n_tok = bsz * rows_out
        outs, dest = _moe(
            tokp.reshape(n_tok, d // 2), route.reshape(n_tok, LANES), cnt[0, :n_experts],
            w_gu, b_gu, w_down, b_down, l)
        new = _combine(outs, dest, wt.reshape(n_tok, LANES), xo.reshape(n_tok, d), modt,
                       g_final, rows_out // COMBINE_TILE, 0 if last else n_ctx // COMBINE_TILE, last)
        new = new.reshape(bsz, rows_out, d)
        if last:
            out = new
        else:
            hcat = new
    return out
```

```python
import functools
import math

import jax
import jax.numpy as jnp
from jax import lax
from jax.experimental import pallas as pl
from jax.experimental.pallas import tpu as pltpu

LRU_HEADS = 8
MLP_HEADS = 8
CONV_W = 4
CONV_LEFT = 2
RG_C = 8.0
CHUNK = 128
TOP_K = 4
SWIGLU_LIMIT = 7.0
SWIGLU_ALPHA = 1.702
EPS = 1e-6

SUBLANES = 8
LANES = 128
TIME_TILE = 256
EXPERT_TILE = 512
DISPATCH_TILE = 128
COMBINE_TILE = 128
WAIT_GROUP = 64
UP_COLS = 1024
UP_COL_CHUNKS = 2
DOWN_COL_CHUNKS = 2
CAST_ROWS = 32
VMEM_LIMIT = 56 * 1024 * 1024
EXPERT_VMEM_LIMIT = 60 * 1024 * 1024

F32 = jnp.float32
BF16 = jnp.bfloat16
U32 = jnp.uint32
HI_MASK = 0xFFFF0000


def _rms(x):
    return x * lax.rsqrt(jnp.mean(x * x, axis=-1, keepdims=True) + EPS)


def _gelu(x):
    c = math.sqrt(2.0 / math.pi)
    return 0.5 * x * (1.0 + jnp.tanh(c * (x + 0.044715 * (x * x * x))))


def _softplus(x):
    return jnp.maximum(x, 0.0) + jnp.log1p(jnp.exp(-jnp.abs(x)))


def _mod_kernel(c_ref, w_ref, b_ref, o_ref):
    c = c_ref[...]
    s = c * jax.nn.sigmoid(c)
    o_ref[0] = jnp.dot(s, w_ref[0], preferred_element_type=F32) + b_ref[0]


def _modulation(c_all, w_mod, b_mod):
    depth, d, n6 = w_mod.shape
    rows = c_all.shape[0]
    tn = n6 // 8
    return pl.pallas_call(
        _mod_kernel,
        grid=(depth, n6 // tn),
        in_specs=[
            pl.BlockSpec((rows, d), lambda l, n: (0, 0)),
            pl.BlockSpec((1, d, tn), lambda l, n: (l, 0, n)),
            pl.BlockSpec((1, 1, tn), lambda l, n: (l, 0, n)),
        ],
        out_specs=pl.BlockSpec((1, rows, tn), lambda l, n: (l, 0, n)),
        out_shape=jax.ShapeDtypeStruct((depth, rows, n6), F32),
        compiler_params=pltpu.CompilerParams(
            dimension_semantics=("arbitrary", "arbitrary"), vmem_limit_bytes=VMEM_LIMIT),
        name="modulation",
    )(c_all, w_mod, b_mod.reshape(depth, 1, n6))


def _inproj_kernel(x_ref, mod_ref, w_ref, z_ref, *, n_chunks):
    x = x_ref[0]
    m = mod_ref[0, 0]
    nx = _rms(x) * (1.0 + m[1:2]) + m[0:1]
    nb = nx.astype(BF16)
    cw = w_ref.shape[1] // n_chunks
    for c in range(n_chunks):
        z_ref[0, :, c * cw:(c + 1) * cw] = jnp.dot(
            nb, w_ref[:, c * cw:(c + 1) * cw], preferred_element_type=F32)


def _inproj(x, modt, w_in_bf16, n_ctx_tiles):
    bsz, s, d = x.shape
    n_cols = w_in_bf16.shape[1]
    tt = TIME_TILE
    return pl.pallas_call(
        functools.partial(_inproj_kernel, n_chunks=4),
        grid=(bsz, s // tt),
        in_specs=[
            pl.BlockSpec((1, tt, d), lambda b, i: (b, i, 0)),
            pl.BlockSpec((1, 1, 6, d), lambda b, i: (b, jnp.where(i < n_ctx_tiles, 0, 1), 0, 0)),
            pl.BlockSpec((d, n_cols), lambda b, i: (0, 0), pipeline_mode=pl.Buffered(1)),
        ],
        out_specs=pl.BlockSpec((1, tt, n_cols), lambda b, i: (b, i, 0)),
        out_shape=jax.ShapeDtypeStruct((bsz, s, n_cols), F32),
        compiler_params=pltpu.CompilerParams(
            dimension_semantics=("parallel", "arbitrary"), vmem_limit_bytes=VMEM_LIMIT),
        name="inproj",
    )(x, modt, w_in_bf16)


def _conv_tile(main, prev8, next8, cw, cb):
    tt = main.shape[0]
    row8 = lax.broadcasted_iota(jnp.int32, (SUBLANES, main.shape[1]), 0)
    w0, w1, w2, w3 = cw[0:1], cw[1:2], cw[2:3], cw[3:4]
    r1 = pltpu.roll(main, 1, 0)
    r2 = pltpu.roll(main, 2, 0)
    rm1 = pltpu.roll(main, tt - 1, 0)
    body = cb + w0 * r2 + w1 * r1 + w2 * main + w3 * rm1
    head = main[0:SUBLANES]
    h1 = jnp.where(row8 < 1, pltpu.roll(prev8, 1, 0), pltpu.roll(head, 1, 0))
    h2 = jnp.where(row8 < 2, pltpu.roll(prev8, 2, 0), pltpu.roll(head, 2, 0))
    head_out = cb + w0 * h2 + w1 * h1 + w2 * head + w3 * rm1[0:SUBLANES]
    tail_m1 = jnp.where(row8 == SUBLANES - 1, pltpu.roll(next8, SUBLANES - 1, 0), rm1[tt - SUBLANES:tt])
    tail_out = (cb + w0 * r2[tt - SUBLANES:tt] + w1 * r1[tt - SUBLANES:tt]
                + w2 * main[tt - SUBLANES:tt] + w3 * tail_m1)
    return body, head_out, tail_out


def _gates_tile(rc_ref, a_ref, u_ref, wg_ref, bg_ref, sp, direction):
    hd = rc_ref.shape[1] // LRU_HEADS
    for h in range(LRU_HEADS):
        cols = slice(h * hd, (h + 1) * hd)
        xh = rc_ref[:, cols]
        pre = jnp.dot(xh.astype(BF16), wg_ref[direction, h], preferred_element_type=F32) + bg_ref[direction, h]
        r = jax.nn.sigmoid(pre[:, :hd])
        gi = jax.nn.sigmoid(pre[:, hd:])
        a = jnp.exp((-RG_C) * r * sp[:, cols])
        a_ref[:, cols] = a
        u_ref[:, cols] = jnp.sqrt(1.0 - a * a) * gi * xh


def _scan_tile(a_ref, u_ref, out_ref, carry_ref, reverse):
    tt, width = a_ref.shape
    groups = tt // SUBLANES
    row8 = lax.broadcasted_iota(jnp.int32, (SUBLANES, width), 0)

    def body(g, carry):
        gi = (groups - 1 - g) if reverse else g
        sl = pl.ds(pl.multiple_of(gi * SUBLANES, SUBLANES), SUBLANES)
        a = a_ref[sl, :]
        u = u_ref[sl, :]
        for k in (1, 2, 4):
            if reverse:
                shift = SUBLANES - k
                valid = row8 < SUBLANES - k
            else:
                shift = k
                valid = row8 >= k
            a_s = pltpu.roll(a, shift, 0)
            u_s = pltpu.roll(u, shift, 0)
            u = jnp.where(valid, a * u_s + u, u)
            a = jnp.where(valid, a * a_s, a)
        h = a * carry + u
        out_ref[0, sl, :] = h
        return h[0:1] if reverse else h[SUBLANES - 1:SUBLANES]

    carry_ref[...] = lax.fori_loop(0, groups, body, carry_ref[...])


def _scan_kernel(zf_ref, zfp_ref, zfn_ref, zb_ref, zbp_ref, zbn_ref, cw_ref, cb_ref, wg_ref, bg_ref,
                 lam_ref, hf_ref, hb_ref, rc_ref, a_ref, u_ref, cf_ref, cbk_ref, *, n_ctx_tiles, n_tiles):
    i = pl.program_id(1)
    tt = rc_ref.shape[0]

    @pl.when(i == 0)
    def _():
        cf_ref[...] = jnp.zeros_like(cf_ref)
        cbk_ref[...] = jnp.zeros_like(cbk_ref)

    jb = jnp.where(i < n_ctx_tiles, n_ctx_tiles - 1 - i, n_tiles - 1 - (i - n_ctx_tiles))
    cw = cw_ref[...]
    cb = cb_ref[...]
    sp = _softplus(-lam_ref[...])

    def run(z_ref, zp_ref, zn_ref, j, direction, out_ref, carry_ref):
        first = jnp.logical_or(j == 0, j == n_ctx_tiles)
        last = jnp.logical_or(j == n_ctx_tiles - 1, j == n_tiles - 1)
        prev8 = zp_ref[0] * jnp.where(first, 0.0, 1.0)
        next8 = zn_ref[0] * jnp.where(last, 0.0, 1.0)
        body, head_out, tail_out = _conv_tile(z_ref[0], prev8, next8, cw, cb)
        rc_ref[...] = body
        rc_ref[0:SUBLANES] = head_out
        rc_ref[tt - SUBLANES:tt] = tail_out
        _gates_tile(rc_ref, a_ref, u_ref, wg_ref, bg_ref, sp[direction:direction + 1], direction)
        _scan_tile(a_ref, u_ref, out_ref, carry_ref, reverse=(direction == 1))

    run(zf_ref, zfp_ref, zfn_ref, i, 0, hf_ref, cf_ref)
    run(zb_ref, zbp_ref, zbn_ref, jb, 1, hb_ref, cbk_ref)


def _scan(z, conv_w, conv_b, wg, bg, lam, n_ctx_tiles):
    bsz, s, _ = z.shape
    r = conv_w.shape[1]
    tt = TIME_TILE
    n_tiles = s // tt
    per = tt // SUBLANES
    last8 = s // SUBLANES - 1

    def bwd_tile(i):
        return jnp.where(i < n_ctx_tiles, n_ctx_tiles - 1 - i, n_tiles - 1 - (i - n_ctx_tiles))

    main_f = pl.BlockSpec((1, tt, r), lambda b, i: (b, i, 0))
    prev_f = pl.BlockSpec((1, SUBLANES, r), lambda b, i: (b, jnp.maximum(i * per - 1, 0), 0))
    next_f = pl.BlockSpec((1, SUBLANES, r), lambda b, i: (b, jnp.minimum((i + 1) * per, last8), 0))
    main_b = pl.BlockSpec((1, tt, r), lambda b, i: (b, bwd_tile(i), 0))
    prev_b = pl.BlockSpec((1, SUBLANES, r), lambda b, i: (b, jnp.maximum(bwd_tile(i) * per - 1, 0), 0))
    next_b = pl.BlockSpec((1, SUBLANES, r), lambda b, i: (b, jnp.minimum((bwd_tile(i) + 1) * per, last8), 0))
    full = lambda shape: pl.BlockSpec(shape, lambda b, i: (0,) * len(shape))
    return pl.pallas_call(
        functools.partial(_scan_kernel, n_ctx_tiles=n_ctx_tiles, n_tiles=n_tiles),
        grid=(bsz, n_tiles),
        in_specs=[main_f, prev_f, next_f, main_b, prev_b, next_b,
                  full(conv_w.shape), full(conv_b.shape), full(wg.shape), full(bg.shape), full(lam.shape)],
        out_specs=[pl.BlockSpec((1, tt, r), lambda b, i: (b, i, 0)),
                   pl.BlockSpec((1, tt, r), lambda b, i: (b, bwd_tile(i), 0))],
        out_shape=[jax.ShapeDtypeStruct((bsz, s, r), F32), jax.ShapeDtypeStruct((bsz, s, r), F32)],
        scratch_shapes=[pltpu.VMEM((tt, r), F32), pltpu.VMEM((tt, r), F32), pltpu.VMEM((tt, r), F32),
                        pltpu.VMEM((1, r), F32), pltpu.VMEM((1, r), F32)],
        compiler_params=pltpu.CompilerParams(
            dimension_semantics=("parallel", "arbitrary"), vmem_limit_bytes=VMEM_LIMIT),
        name="lru_scan",
    )(z, z, z, z, z, z, conv_w, conv_b, wg, bg, lam)


def _finish_kernel(zg_ref, zu_ref, zv_ref, hf_ref, hb_ref, x_ref, mod_ref, glru_ref, gmlp_ref, gv_ref,
                   ws_ref, bs_ref, wout_ref, wr_ref, br_ref,
                   xo_ref, tokp_ref, route_ref, wt_ref, cnt_ref, ym_ref, run_ref, *, n_experts):
    tt, r = ym_ref.shape
    hd = r // MLP_HEADS

    @pl.when(jnp.logical_and(pl.program_id(0) == 0, pl.program_id(1) == 0))
    def _():
        run_ref[...] = jnp.zeros_like(run_ref)

    y_lru = (hf_ref[0] + hb_ref[0]) * _gelu(zg_ref[0])
    yl = _rms(y_lru) * glru_ref[...]
    u = _gelu(zu_ref[0])
    vb = (_rms(_gelu(zv_ref[0])) * gv_ref[...]).astype(BF16)
    for ch in range(tt // CHUNK):
        rows = slice(ch * CHUNK, (ch + 1) * CHUNK)
        for h in range(MLP_HEADS):
            cols = slice(h * hd, (h + 1) * hd)
            mixed = jnp.dot(ws_ref[h], vb[rows, cols], preferred_element_type=F32) + bs_ref[h]
            ym_ref[rows, cols] = u[rows, cols] * mixed
    ym = _rms(ym_ref[...]) * gmlp_ref[...]
    y = (jnp.dot(yl.astype(BF16), wout_ref[0:r], preferred_element_type=F32)
         + jnp.dot(ym.astype(BF16), wout_ref[r:2 * r], preferred_element_type=F32))
    m = mod_ref[0, 0]
    xnew = x_ref[0] + m[2:3] * y
    xo_ref[0] = xnew
    tok = _rms(xnew) * (1.0 + m[4:5]) + m[3:4]

    half = tok.shape[1] // 2
    t_hi = tok.astype(BF16)
    t_hi32 = t_hi.astype(F32)
    bits = lax.bitcast_convert_type(t_hi32, U32)
    tokp_ref[0] = bits[:, :half] | (bits[:, half:] >> 16)

    t_lo = (tok - t_hi32).astype(BF16)
    logits = (jnp.dot(t_hi, wr_ref[0], preferred_element_type=F32)
              + jnp.dot(t_lo, wr_ref[0], preferred_element_type=F32)
              + jnp.dot(t_hi, wr_ref[1], preferred_element_type=F32)) + br_ref[...]
    lane = lax.broadcasted_iota(jnp.int32, logits.shape, 1)
    neg = jnp.float32(-jnp.inf)
    work = jnp.where(lane < n_experts, logits, neg)
    vals, idxs = [], []
    for _ in range(TOP_K):
        mx = jnp.max(work, axis=-1, keepdims=True)
        ix = jnp.min(jnp.where(work == mx, lane, LANES), axis=-1, keepdims=True)
        vals.append(mx)
        idxs.append(ix)
        work = jnp.where(lane == ix, neg, work)
    exps = [jnp.exp(v - vals[0]) for v in vals]
    denom = exps[0] + exps[1] + exps[2] + exps[3]

    hot = [lane == ix for ix in idxs]
    multi = jnp.zeros(logits.shape, F32)
    for k in range(TOP_K):
        multi = multi + hot[k].astype(F32)
    before = (lax.broadcasted_iota(jnp.int32, (tt, tt), 1) < lax.broadcasted_iota(jnp.int32, (tt, tt), 0))
    excl = jnp.dot(before.astype(BF16), multi.astype(BF16), preferred_element_type=F32) + run_ref[...]
    run_ref[...] = run_ref[...] + jnp.sum(multi, axis=0, keepdims=True)
    cnt_ref[...] = jnp.broadcast_to(run_ref[...], cnt_ref.shape).astype(jnp.int32)

    route = jnp.zeros(logits.shape, jnp.int32)
    wt_out = jnp.zeros(logits.shape, F32)
    for k in range(TOP_K):
        rank = jnp.sum(jnp.where(hot[k], excl, 0.0), axis=-1, keepdims=True).astype(jnp.int32)
        route = jnp.where(lane == k, idxs[k], route)
        route = jnp.where(lane == TOP_K + k, rank, route)
        wt_out = jnp.where(lane == k, exps[k] / denom, wt_out)
    route_ref[0] = route
    wt_ref[0] = wt_out


def _finish(z, hf, hb, x, modt, g_lru, g_mlp, g_v, ws, bs, w_out_bf16, wr_split, br_pad, n_experts,
            tile_offset, n_ctx_tiles):
    bsz, s, d = x.shape
    r = hf.shape[2]
    tt = TIME_TILE
    n_out_tiles = s // tt - tile_offset
    rows_out = n_out_tiles * tt
    off = tile_offset

    def zcol(c):
        return pl.BlockSpec((1, tt, r), lambda b, i: (b, i + off, c))

    full = lambda shape: pl.BlockSpec(shape, lambda b, i: (0,) * len(shape))
    tile_in = lambda w: pl.BlockSpec((1, tt, w), lambda b, i: (b, i + off, 0))
    tile_out = lambda w: pl.BlockSpec((1, tt, w), lambda b, i: (b, i, 0))
    return pl.pallas_call(
        functools.partial(_finish_kernel, n_experts=n_experts),
        grid=(bsz, n_out_tiles),
        in_specs=[zcol(1), zcol(2), zcol(3), tile_in(r), tile_in(r), tile_in(d),
                  pl.BlockSpec((1, 1, 6, d), lambda b, i: (b, jnp.where(i + off < n_ctx_tiles, 0, 1), 0, 0)),
                  full(g_lru.shape), full(g_mlp.shape), full(g_v.shape), full(ws.shape), full(bs.shape),
                  pl.BlockSpec(w_out_bf16.shape, lambda b, i: (0, 0), pipeline_mode=pl.Buffered(1)),
                  full(wr_split.shape), full(br_pad.shape)],
        out_specs=[tile_out(d), tile_out(d // 2), tile_out(LANES), tile_out(LANES),
                   pl.BlockSpec((SUBLANES, LANES), lambda b, i: (0, 0))],
        out_shape=[jax.ShapeDtypeStruct((bsz, rows_out, d), F32),
                   jax.ShapeDtypeStruct((bsz, rows_out, d // 2), U32),
                   jax.ShapeDtypeStruct((bsz, rows_out, LANES), jnp.int32),
                   jax.ShapeDtypeStruct((bsz, rows_out, LANES), F32),
                   jax.ShapeDtypeStruct((SUBLANES, LANES), jnp.int32)],
        scratch_shapes=[pltpu.VMEM((tt, r), F32), pltpu.VMEM((1, LANES), F32)],
        compiler_params=pltpu.CompilerParams(
            dimension_semantics=("arbitrary", "arbitrary"), vmem_limit_bytes=VMEM_LIMIT),
        name="mixer_finish",
    )(z, z, z, hf, hb, x, modt, g_lru, g_mlp, g_v, ws, bs, w_out_bf16, wr_split, br_pad)


def _row_copy(src_ref, src_row, dst_ref, dst_row, sem):
    return pltpu.make_async_copy(src_ref.at[pl.ds(src_row, 1)], dst_ref.at[pl.ds(dst_row, 1)], sem)


def _wait_rows(src_ref, dst_ref, sem, n_rows):
    def group(_, carry):
        for _ in range(WAIT_GROUP):
            _row_copy(src_ref, 0, dst_ref, 0, sem).wait()
        return carry

    lax.fori_loop(0, n_rows // WAIT_GROUP, group, 0)


def _zero_pieces(tile):
    return [1 << b for b in range(tile.bit_length() - 2, 2, -1)]


def _dispatch_kernel(zstart_ref, zlen_ref, dest_ref, tok_ref, xs_ref, zbuf, sem, zsem, *, n_experts):
    s = pl.program_id(0)
    td = dest_ref.shape[2] // TOP_K
    pieces = _zero_pieces(EXPERT_TILE)

    @pl.when(s == 0)
    def _():
        zbuf[...] = jnp.zeros_like(zbuf)

        def fill(e, carry):
            gap = zlen_ref[e]
            start = zstart_ref[e]
            lead = jnp.minimum((SUBLANES - (start & (SUBLANES - 1))) & (SUBLANES - 1), gap)
            for i in range(SUBLANES - 1):
                @pl.when(i < lead)
                def _():
                    _row_copy(zbuf, 0, xs_ref, start + i, zsem).start()
            rest = gap - lead
            off = start + lead
            for p in pieces:
                @pl.when((rest & p) != 0)
                def _():
                    pltpu.make_async_copy(zbuf.at[pl.ds(0, p)],
                                          xs_ref.at[pl.ds(pl.multiple_of(off, SUBLANES), p)], zsem).start()

                off = off + (rest & p)
            for i in range(SUBLANES - 1):
                @pl.when(i < lead)
                def _():
                    _row_copy(zbuf, 0, xs_ref, 0, zsem).wait()
            for p in pieces:
                @pl.when((rest & p) != 0)
                def _():
                    pltpu.make_async_copy(zbuf.at[pl.ds(0, p)], xs_ref.at[pl.ds(0, p)], zsem).wait()
            return carry

        lax.fori_loop(0, n_experts, fill, 0)

        zrows = zbuf.shape[0]
        first_free = (zstart_ref[n_experts - 1] + zlen_ref[n_experts - 1]) // zrows
        n_chunks = xs_ref.shape[0] // zrows

        def fill_tail(c, carry):
            row = pl.multiple_of(c * zrows, zrows)
            cp = pltpu.make_async_copy(zbuf, xs_ref.at[pl.ds(row, zrows)], zsem)
            cp.start()
            cp.wait()
            return carry

        lax.fori_loop(first_free, n_chunks, fill_tail, 0)

    def issue(t, carry):
        for k in range(TOP_K):
            _row_copy(tok_ref, t, xs_ref, dest_ref[0, 0, t * TOP_K + k], sem).start()
        return carry

    lax.fori_loop(0, td, issue, 0, unroll=4)
    _wait_rows(tok_ref, xs_ref, sem, td * TOP_K)


def _dispatch(tokp, dest, zstart, zlen, n_rows):
    n_tok, dh = tokp.shape
    td = DISPATCH_TILE
    n_steps = n_tok // td
    n_experts = zstart.shape[0]
    zrows = _zero_pieces(EXPERT_TILE)[0]
    return pl.pallas_call(
        functools.partial(_dispatch_kernel, n_experts=n_experts),
        grid_spec=pltpu.PrefetchScalarGridSpec(
            num_scalar_prefetch=2,
            grid=(n_steps,),
            in_specs=[pl.BlockSpec((1, 1, td * TOP_K), lambda s, a, b: (s, 0, 0), memory_space=pltpu.SMEM),
                      pl.BlockSpec((td, dh), lambda s, a, b: (s, 0))],
            out_specs=pl.BlockSpec(memory_space=pl.ANY),
            scratch_shapes=[pltpu.VMEM((zrows, dh), U32), pltpu.SemaphoreType.DMA(()),
                            pltpu.SemaphoreType.DMA(())],
        ),
        out_shape=jax.ShapeDtypeStruct((n_rows, dh), U32),
        compiler_params=pltpu.CompilerParams(dimension_semantics=("arbitrary",)),
        name="expert_dispatch",
    )(zstart, zlen, dest.reshape(n_steps, 1, td * TOP_K), tokp)


def _cast_rows(staging_ref, slot, resident_ref):
    def body(c, carry):
        rows = pl.ds(pl.multiple_of(c * CAST_ROWS, CAST_ROWS), CAST_ROWS)
        resident_ref[rows, :] = staging_ref[slot, rows, :].astype(BF16)
        return carry

    lax.fori_loop(0, resident_ref.shape[0] // CAST_ROWS, body, 0)


def _stream_expert_weights(be_ref, ord_ref, nxt_ref, meta_ref, n_sweeps, start, wait, cast):
    n = pl.program_id(0)
    j = pl.program_id(1)
    e = be_ref[j]
    used = j < meta_ref[0]
    changed = jnp.logical_and(used, jnp.logical_or(j == 0, e != be_ref[jnp.maximum(j - 1, 0)]))
    slot = (n * meta_ref[1] + ord_ref[j]) % 2

    @pl.when(jnp.logical_and(n == 0, j == 0))
    def _():
        start(e, n, slot)

    @pl.when(changed)
    def _():
        wait(e, n, slot)
        cast(slot)
        following = nxt_ref[j]

        @pl.when(following >= 0)
        def _():
            start(following, n, 1 - slot)

        @pl.when(jnp.logical_and(following < 0, n + 1 < n_sweeps))
        def _():
            start(be_ref[0], n + 1, 1 - slot)

    return used


def _expert_up_kernel(be_ref, ord_ref, nxt_ref, meta_ref, x_ref, wgu_ref, bgate_ref, bup_ref, h_ref,
                      wbuf_g, wbuf_u, wg_s, wu_s, sem, *, layer, n_sweeps):
    tn = h_ref.shape[1]

    def copies(e, n, slot):
        gate_cols = pl.ds(pl.multiple_of(n * tn, tn), tn)
        up_cols = pl.ds(pl.multiple_of((n_sweeps + n) * tn, tn), tn)
        return (pltpu.make_async_copy(wgu_ref.at[layer, e, :, gate_cols], wbuf_g.at[slot], sem.at[0, slot]),
                pltpu.make_async_copy(wgu_ref.at[layer, e, :, up_cols], wbuf_u.at[slot], sem.at[1, slot]))

    def start(e, n, slot):
        for cp in copies(e, n, slot):
            cp.start()

    def wait(e, n, slot):
        for cp in copies(e, n, slot):
            cp.wait()

    def cast(slot):
        _cast_rows(wbuf_g, slot, wg_s)
        _cast_rows(wbuf_u, slot, wu_s)

    used = _stream_expert_weights(be_ref, ord_ref, nxt_ref, meta_ref, n_sweeps, start, wait, cast)

    @pl.when(used)
    def _():
        xp = x_ref[...]
        half = xp.shape[1]
        xa = lax.bitcast_convert_type(xp & jnp.uint32(HI_MASK), F32).astype(BF16)
        xb = lax.bitcast_convert_type(xp << 16, F32).astype(BF16)
        tn = h_ref.shape[1]
        cw = tn // UP_COL_CHUNKS
        for c in range(UP_COL_CHUNKS):
            cols = slice(c * cw, (c + 1) * cw)
            gate = (jnp.dot(xa, wg_s[0:half, cols], preferred_element_type=F32)
                    + jnp.dot(xb, wg_s[half:2 * half, cols], preferred_element_type=F32)) + bgate_ref[0, 0, :, cols]
            up = (jnp.dot(xa, wu_s[0:half, cols], preferred_element_type=F32)
                  + jnp.dot(xb, wu_s[half:2 * half, cols], preferred_element_type=F32)) + bup_ref[0, 0, :, cols]
            gate = jnp.minimum(gate, SWIGLU_LIMIT)
            up = jnp.clip(up, -SWIGLU_LIMIT, SWIGLU_LIMIT)
            act = (up + 1.0) * gate * jax.nn.sigmoid(SWIGLU_ALPHA * gate)
            h_ref[:, cols] = act.astype(BF16)

    @pl.when(jnp.logical_not(used))
    def _():
        h_ref[...] = jnp.zeros_like(h_ref)


def _expert_up(xs, w_gu, b_gu, layer, plan):
    n_rows, dh = xs.shape
    d = 2 * dh
    _, n_exp, _, two_f = w_gu.shape
    f = two_f // 2
    tm = EXPERT_TILE
    tn = UP_COLS
    n_blocks = n_rows // tm
    nf = f // tn
    b4 = b_gu.reshape(b_gu.shape[0], n_exp, 1, two_f)
    return pl.pallas_call(
        functools.partial(_expert_up_kernel, layer=layer, n_sweeps=nf),
        grid_spec=pltpu.PrefetchScalarGridSpec(
            num_scalar_prefetch=4,
            grid=(nf, n_blocks),
            in_specs=[pl.BlockSpec((tm, dh), lambda n, j, be, od, nx, mt: (j, 0)),
                      pl.BlockSpec(memory_space=pl.ANY),
                      pl.BlockSpec((1, 1, 1, tn), lambda n, j, be, od, nx, mt: (layer, be[j], 0, n)),
                      pl.BlockSpec((1, 1, 1, tn), lambda n, j, be, od, nx, mt: (layer, be[j], 0, nf + n))],
            out_specs=pl.BlockSpec((tm, tn), lambda n, j, be, od, nx, mt: (j, n)),
            scratch_shapes=[pltpu.VMEM((2, d, tn), F32), pltpu.VMEM((2, d, tn), F32),
                            pltpu.VMEM((d, tn), BF16), pltpu.VMEM((d, tn), BF16),
                            pltpu.SemaphoreType.DMA((2, 2))],
        ),
        out_shape=jax.ShapeDtypeStruct((n_rows, f), BF16),
        compiler_params=pltpu.CompilerParams(
            dimension_semantics=("arbitrary", "arbitrary"), vmem_limit_bytes=EXPERT_VMEM_LIMIT),
        name="expert_up",
    )(*plan, xs, w_gu, b4, b4)


def _expert_down_kernel(be_ref, ord_ref, nxt_ref, meta_ref, h_ref, wd_ref, bd_ref, o_ref, wbuf, wd_s, sem,
                        *, layer):
    def copy(e, slot):
        return pltpu.make_async_copy(wd_ref.at[layer, e], wbuf.at[slot], sem.at[slot])

    def cast(slot):
        _cast_rows(wbuf, slot, wd_s)

    used = _stream_expert_weights(be_ref, ord_ref, nxt_ref, meta_ref, 1,
                                  lambda e, n, slot: copy(e, slot).start(),
                                  lambda e, n, slot: copy(e, slot).wait(), cast)

    @pl.when(used)
    def _():
        hb = h_ref[...]
        half = o_ref.shape[1]
        cw = half // DOWN_COL_CHUNKS
        for c in range(DOWN_COL_CHUNKS):
            lo = slice(c * cw, (c + 1) * cw)
            hi = slice(half + c * cw, half + (c + 1) * cw)
            ya = jnp.dot(hb, wd_s[:, lo], preferred_element_type=F32) + bd_ref[0, 0, :, lo]
            yb = jnp.dot(hb, wd_s[:, hi], preferred_element_type=F32) + bd_ref[0, 0, :, hi]
            ba = lax.bitcast_convert_type(ya.astype(BF16).astype(F32), U32)
            bb = lax.bitcast_convert_type(yb.astype(BF16).astype(F32), U32)
            o_ref[:, lo] = ba | (bb >> 16)

    @pl.when(jnp.logical_not(used))
    def _():
        o_ref[...] = jnp.zeros_like(o_ref)


def _expert_down(h, w_down, b_down, layer, plan):
    n_rows, f = h.shape
    _, n_exp, _, d = w_down.shape
    tm = EXPERT_TILE
    n_blocks = n_rows // tm
    return pl.pallas_call(
        functools.partial(_expert_down_kernel, layer=layer),
        grid_spec=pltpu.PrefetchScalarGridSpec(
            num_scalar_prefetch=4,
            grid=(1, n_blocks),
            in_specs=[pl.BlockSpec((tm, f), lambda n, j, be, od, nx, mt: (j, 0)),
                      pl.BlockSpec(memory_space=pl.ANY),
                      pl.BlockSpec((1, 1, 1, d), lambda n, j, be, od, nx, mt: (layer, be[j], 0, 0))],
            out_specs=pl.BlockSpec((tm, d // 2), lambda n, j, be, od, nx, mt: (j, 0)),
            scratch_shapes=[pltpu.VMEM((2, f, d), F32), pltpu.VMEM((f, d), BF16), pltpu.SemaphoreType.DMA((2,))],
        ),
        out_shape=jax.ShapeDtypeStruct((n_rows, d // 2), U32),
        compiler_params=pltpu.CompilerParams(
            dimension_semantics=("arbitrary", "arbitrary"), vmem_limit_bytes=EXPERT_VMEM_LIMIT),
        name="expert_down",
    )(*plan, h, w_down, b_down.reshape(b_down.shape[0], n_exp, 1, d))


def _combine_kernel(destc_ref, destn_ref, outs_ref, w_ref, x_ref, mod_ref, gfin_ref, o_ref, buf, sem, *, final):
    t = pl.program_id(0)
    nt = pl.num_programs(0)
    tb = x_ref.shape[0]

    def issue(dest_ref, slot):
        def body(r, carry):
            for k in range(TOP_K):
                p = dest_ref[0, 0, r * TOP_K + k]
                pltpu.make_async_copy(outs_ref.at[pl.ds(p, 1)], buf.at[slot, k, pl.ds(r, 1)], sem.at[slot]).start()
            return carry

        lax.fori_loop(0, tb, body, 0, unroll=4)

    @pl.when(t == 0)
    def _():
        issue(destc_ref, 0)

    @pl.when(t + 1 < nt)
    def _():
        issue(destn_ref, (t + 1) % 2)

    slot = t % 2

    def group(_, carry):
        for _ in range(WAIT_GROUP):
            pltpu.make_async_copy(outs_ref.at[pl.ds(0, 1)], buf.at[slot, 0, pl.ds(0, 1)], sem.at[slot]).wait()
        return carry

    lax.fori_loop(0, tb * TOP_K // WAIT_GROUP, group, 0)

    w = w_ref[...]
    half = buf.shape[3]
    ya = jnp.zeros((tb, half), F32)
    yb = jnp.zeros((tb, half), F32)
    for k in range(TOP_K):
        p = buf[slot, k]
        wk = w[:, k:k + 1]
        ya = ya + wk * lax.bitcast_convert_type(p & jnp.uint32(HI_MASK), F32)
        yb = yb + wk * lax.bitcast_convert_type(p << 16, F32)
    m = mod_ref[0, 0]
    xa = x_ref[:, 0:half] + m[5:6, 0:half] * ya
    xb = x_ref[:, half:2 * half] + m[5:6, half:2 * half] * yb
    if final:
        ms = (jnp.sum(xa * xa, axis=-1, keepdims=True) + jnp.sum(xb * xb, axis=-1, keepdims=True)) / (2 * half)
        scale = lax.rsqrt(ms + EPS)
        xa = xa * scale * gfin_ref[:, 0:half]
        xb = xb * scale * gfin_ref[:, half:2 * half]
    o_ref[:, 0:half] = xa
    o_ref[:, half:2 * half] = xb


def _combine(outs, dest, wt, x_flat, modt, g_final, tiles_per_batch, n_ctx_tiles, final):
    n_tok, d = x_flat.shape
    tb = COMBINE_TILE
    n_tiles = n_tok // tb
    cur = lambda t: (t, 0, 0)
    nxt = lambda t: (jnp.minimum(t + 1, n_tiles - 1), 0, 0)
    smem = lambda imap: pl.BlockSpec((1, 1, tb * TOP_K), imap, memory_space=pltpu.SMEM)
    dest3 = dest.reshape(n_tiles, 1, tb * TOP_K)
    return pl.pallas_call(
        functools.partial(_combine_kernel, final=final),
        grid=(n_tiles,),
        in_specs=[smem(cur), smem(nxt),
                  pl.BlockSpec(memory_space=pl.ANY),
                  pl.BlockSpec((tb, LANES), lambda t: (t, 0)),
                  pl.BlockSpec((tb, d), lambda t: (t, 0)),
                  pl.BlockSpec((1, 1, 6, d), lambda t: (
                      t // tiles_per_batch, jnp.where(t % tiles_per_batch < n_ctx_tiles, 0, 1), 0, 0)),
                  pl.BlockSpec((1, d), lambda t: (0, 0))],
        out_specs=pl.BlockSpec((tb, d), lambda t: (t, 0)),
        out_shape=jax.ShapeDtypeStruct((n_tok, d), F32),
        scratch_shapes=[pltpu.VMEM((2, TOP_K, tb, d // 2), U32), pltpu.SemaphoreType.DMA((2,))],
        compiler_params=pltpu.CompilerParams(
            dimension_semantics=("arbitrary",), vmem_limit_bytes=VMEM_LIMIT),
        name="expert_combine",
    )(dest3, dest3, outs, wt, x_flat, modt, g_final.reshape(1, d))


def _expert_layout(counts, n_assign):
    n_experts = counts.shape[0]
    tm = EXPERT_TILE
    padded = (counts + tm - 1) // tm * tm
    ends_p = jnp.cumsum(padded).astype(jnp.int32)
    starts_p = ends_p - padded
    n_blocks = (n_assign + n_experts * (tm - 1) + tm - 1) // tm
    block_start = jnp.arange(n_blocks, dtype=jnp.int32) * tm
    block_e = jnp.minimum(jnp.sum((ends_p[None, :] <= block_start[:, None]).astype(jnp.int32), axis=1),
                          n_experts - 1).astype(jnp.int32)
    n_used = ends_p[-1] // tm
    tile = jnp.arange(n_blocks, dtype=jnp.int32)
    used = tile < n_used
    first = jnp.logical_and(used, jnp.concatenate([jnp.ones((1,), bool), block_e[1:] != block_e[:-1]]))
    ordinal = (jnp.cumsum(first.astype(jnp.int32)) - 1).astype(jnp.int32)
    first_pos = jnp.where(first, tile, n_blocks)
    after = jnp.concatenate([first_pos[1:], jnp.full((1,), n_blocks, jnp.int32)])
    next_first = lax.cummin(after, axis=0, reverse=True)
    following = jnp.where(next_first < n_blocks, block_e[jnp.minimum(next_first, n_blocks - 1)], -1).astype(jnp.int32)
    meta = jnp.stack([n_used, jnp.sum(first.astype(jnp.int32))]).astype(jnp.int32)
    plan = (block_e, ordinal, following, meta)
    return starts_p, starts_p + counts, padded - counts, plan, n_blocks * tm


def _moe(tokp, route, counts, w_gu, b_gu, w_down, b_down, layer):
    n_tok = tokp.shape[0]
    n_experts = counts.shape[0]
    idx4 = route[:, :TOP_K]
    rank4 = route[:, TOP_K:2 * TOP_K]
    starts_p, zstart, zlen, plan, n_rows = _expert_layout(counts, n_tok * TOP_K)
    onehot = idx4[:, :, None] == jnp.arange(n_experts, dtype=jnp.int32)[None, None, :]
    dest = (jnp.sum(jnp.where(onehot, starts_p[None, None, :], 0), axis=-1) + rank4).astype(jnp.int32)
    xs = _dispatch(tokp, dest, zstart, zlen, n_rows)
    h = _expert_up(xs, w_gu, b_gu, layer, plan)
    outs = _expert_down(h, w_down, b_down, layer, plan)
    return outs, dest


def kernel(x, c, ctx, c_ctx, w_mod, b_mod, w_in, conv_w, conv_b, w_r, b_r, w_i, b_i, lam, g_v, w_s, b_s,
           g_lru, g_mlp, w_out, w_router, b_router, w_gu, b_gu, w_down, b_down, g_final):
    bsz, n_lat, d = x.shape
    n_ctx = ctx.shape[1]
    depth = w_mod.shape[0]
    n_experts = w_router.shape[2]
    r = conv_w.shape[2]
    tt = TIME_TILE
    assert n_ctx % tt == 0 and n_lat % tt == 0 and tt % CHUNK == 0
    assert tt % DISPATCH_TILE == 0 and tt % COMBINE_TILE == 0
    assert n_experts <= LANES and r % LRU_HEADS == 0
    n_ctx_tiles = n_ctx // tt

    rows = (bsz + 1 + SUBLANES - 1) // SUBLANES * SUBLANES
    c_all = jnp.zeros((rows, d), F32).at[:bsz].set(c).at[bsz].set(c_ctx)
    mod = _modulation(c_all, w_mod, b_mod)

    hcat = jnp.concatenate([ctx, x], axis=1)
    out = None
    for l in range(depth):
        last = l == depth - 1
        mod_x = mod[l, :bsz].reshape(bsz, 1, 6, d)
        mod_c = jnp.broadcast_to(mod[l, bsz].reshape(1, 1, 6, d), (bsz, 1, 6, d))
        modt = jnp.concatenate([mod_c, mod_x], axis=1)

        z = _inproj(hcat, modt, w_in[l].astype(BF16), n_ctx_tiles)
        hd = r // LRU_HEADS
        wg = jnp.concatenate([w_r[l], w_i[l]], axis=-1).astype(BF16)
        bg = jnp.concatenate([b_r[l], b_i[l]], axis=-1).reshape(2, LRU_HEADS, 1, 2 * hd)
        hf, hb = _scan(z, conv_w[l], conv_b[l].reshape(1, r), wg, bg, lam[l], n_ctx_tiles)

        wr_hi = w_router[l].astype(BF16)
        wr_lo = (w_router[l] - wr_hi.astype(F32)).astype(BF16)
        wr_split = jnp.pad(jnp.stack([wr_hi, wr_lo]), ((0, 0), (0, 0), (0, LANES - n_experts)))
        br_pad = jnp.pad(b_router[l], (0, LANES - n_experts)).reshape(1, LANES)
        off = n_ctx_tiles if last else 0
        xo, tokp, route, wt, cnt = _finish(
            z, hf, hb, hcat, modt, g_lru[l].reshape(1, r), g_mlp[l].reshape(1, r), g_v[l].reshape(1, r),
            w_s[l].astype(BF16), b_s[l].reshape(MLP_HEADS, CHUNK, 1), w_out[l].astype(BF16),
            wr_split, br_pad, n_experts, off, n_ctx_tiles)

        rows_out = xo.shape[1]
        n_tok = bsz * rows_out
        outs, dest = _moe(
            tokp.reshape(n_tok, d // 2), route.reshape(n_tok, LANES), cnt[0, :n_experts],
            w_gu, b_gu, w_down, b_down, l)
        new = _combine(outs, dest, wt.reshape(n_tok, LANES), xo.reshape(n_tok, d), modt,
                       g_final, rows_out // COMBINE_TILE, 0 if last else n_ctx // COMBINE_TILE, last)
        new = new.reshape(bsz, rows_out, d)
        if last:
            out = new
        else:
            hcat = new
    return out
```

```python
import functools
import math

import jax
import jax.numpy as jnp
from jax import lax
from jax.experimental import pallas as pl
from jax.experimental.pallas import tpu as pltpu

LRU_HEADS = 8
MLP_HEADS = 8
CONV_W = 4
CONV_LEFT = 2
RG_C = 8.0
CHUNK = 128
TOP_K = 4
SWIGLU_LIMIT = 7.0
SWIGLU_ALPHA = 1.702
EPS = 1e-6

SUBLANES = 8
LANES = 128
TIME_TILE = 256
EXPERT_TILE = 512
DISPATCH_TILE = 128
COMBINE_TILE = 128
WAIT_GROUP = 64
UP_COLS = 1024
UP_COL_CHUNKS = 2
DOWN_COL_CHUNKS = 2
CAST_ROWS = 32
VMEM_LIMIT = 56 * 1024 * 1024
EXPERT_VMEM_LIMIT = 60 * 1024 * 1024

F32 = jnp.float32
BF16 = jnp.bfloat16
U32 = jnp.uint32
HI_MASK = 0xFFFF0000


def _rms(x):
    return x * lax.rsqrt(jnp.mean(x * x, axis=-1, keepdims=True) + EPS)


def _gelu(x):
    c = math.sqrt(2.0 / math.pi)
    return 0.5 * x * (1.0 + jnp.tanh(c * (x + 0.044715 * (x * x * x))))


def _softplus(x):
    return jnp.maximum(x, 0.0) + jnp.log1p(jnp.exp(-jnp.abs(x)))


def _mod_kernel(c_ref, w_ref, b_ref, o_ref):
    c = c_ref[...]
    s = c * jax.nn.sigmoid(c)
    o_ref[0] = jnp.dot(s, w_ref[0], preferred_element_type=F32) + b_ref[0]


def _modulation(c_all, w_mod, b_mod):
    depth, d, n6 = w_mod.shape
    rows = c_all.shape[0]
    tn = n6 // 8
    return pl.pallas_call(
        _mod_kernel,
        grid=(depth, n6 // tn),
        in_specs=[
            pl.BlockSpec((rows, d), lambda l, n: (0, 0)),
            pl.BlockSpec((1, d, tn), lambda l, n: (l, 0, n)),
            pl.BlockSpec((1, 1, tn), lambda l, n: (l, 0, n)),
        ],
        out_specs=pl.BlockSpec((1, rows, tn), lambda l, n: (l, 0, n)),
        out_shape=jax.ShapeDtypeStruct((depth, rows, n6), F32),
        compiler_params=pltpu.CompilerParams(
            dimension_semantics=("arbitrary", "arbitrary"), vmem_limit_bytes=VMEM_LIMIT),
        name="modulation",
    )(c_all, w_mod, b_mod.reshape(depth, 1, n6))


def _inproj_kernel(x_ref, mod_ref, w_ref, z_ref, *, n_chunks):
    x = x_ref[0]
    m = mod_ref[0, 0]
    nx = _rms(x) * (1.0 + m[1:2]) + m[0:1]
    nb = nx.astype(BF16)
    cw = w_ref.shape[1] // n_chunks
    for c in range(n_chunks):
        z_ref[0, :, c * cw:(c + 1) * cw] = jnp.dot(
            nb, w_ref[:, c * cw:(c + 1) * cw], preferred_element_type=F32)


def _inproj(x, modt, w_in_bf16, n_ctx_tiles):
    bsz, s, d = x.shape
    n_cols = w_in_bf16.shape[1]
    tt = TIME_TILE
    return pl.pallas_call(
        functools.partial(_inproj_kernel, n_chunks=4),
        grid=(bsz, s // tt),
        in_specs=[
            pl.BlockSpec((1, tt, d), lambda b, i: (b, i, 0)),
            pl.BlockSpec((1, 1, 6, d), lambda b, i: (b, jnp.where(i < n_ctx_tiles, 0, 1), 0, 0)),
            pl.BlockSpec((d, n_cols), lambda b, i: (0, 0), pipeline_mode=pl.Buffered(1)),
        ],
        out_specs=pl.BlockSpec((1, tt, n_cols), lambda b, i: (b, i, 0)),
        out_shape=jax.ShapeDtypeStruct((bsz, s, n_cols), F32),
        compiler_params=pltpu.CompilerParams(
            dimension_semantics=("parallel", "arbitrary"), vmem_limit_bytes=VMEM_LIMIT),
        name="inproj",
    )(x, modt, w_in_bf16)


def _conv_tile(main, prev8, next8, cw, cb):
    tt = main.shape[0]
    row8 = lax.broadcasted_iota(jnp.int32, (SUBLANES, main.shape[1]), 0)
    w0, w1, w2, w3 = cw[0:1], cw[1:2], cw[2:3], cw[3:4]
    r1 = pltpu.roll(main, 1, 0)
    r2 = pltpu.roll(main, 2, 0)
    rm1 = pltpu.roll(main, tt - 1, 0)
    body = cb + w0 * r2 + w1 * r1 + w2 * main + w3 * rm1
    head = main[0:SUBLANES]
    h1 = jnp.where(row8 < 1, pltpu.roll(prev8, 1, 0), pltpu.roll(head, 1, 0))
    h2 = jnp.where(row8 < 2, pltpu.roll(prev8, 2, 0), pltpu.roll(head, 2, 0))
    head_out = cb + w0 * h2 + w1 * h1 + w2 * head + w3 * rm1[0:SUBLANES]
    tail_m1 = jnp.where(row8 == SUBLANES - 1, pltpu.roll(next8, SUBLANES - 1, 0), rm1[tt - SUBLANES:tt])
    tail_out = (cb + w0 * r2[tt - SUBLANES:tt] + w1 * r1[tt - SUBLANES:tt]
                + w2 * main[tt - SUBLANES:tt] + w3 * tail_m1)
    return body, head_out, tail_out


def _gates_tile(rc_ref, a_ref, u_ref, wg_ref, bg_ref, sp, direction):
    hd = rc_ref.shape[1] // LRU_HEADS
    for h in range(LRU_HEADS):
        cols = slice(h * hd, (h + 1) * hd)
        xh = rc_ref[:, cols]
        pre = jnp.dot(xh.astype(BF16), wg_ref[direction, h], preferred_element_type=F32) + bg_ref[direction, h]
        r = jax.nn.sigmoid(pre[:, :hd])
        gi = jax.nn.sigmoid(pre[:, hd:])
        a = jnp.exp((-RG_C) * r * sp[:, cols])
        a_ref[:, cols] = a
        u_ref[:, cols] = jnp.sqrt(1.0 - a * a) * gi * xh


def _scan_tile(a_ref, u_ref, out_ref, carry_ref, reverse):
    tt, width = a_ref.shape
    groups = tt // SUBLANES
    row8 = lax.broadcasted_iota(jnp.int32, (SUBLANES, width), 0)

    def body(g, carry):
        gi = (groups - 1 - g) if reverse else g
        sl = pl.ds(pl.multiple_of(gi * SUBLANES, SUBLANES), SUBLANES)
        a = a_ref[sl, :]
        u = u_ref[sl, :]
        for k in (1, 2, 4):
            if reverse:
                shift = SUBLANES - k
                valid = row8 < SUBLANES - k
            else:
                shift = k
                valid = row8 >= k
            a_s = pltpu.roll(a, shift, 0)
            u_s = pltpu.roll(u, shift, 0)
            u = jnp.where(valid, a * u_s + u, u)
            a = jnp.where(valid, a * a_s, a)
        h = a * carry + u
        out_ref[0, sl, :] = h
        return h[0:1] if reverse else h[SUBLANES - 1:SUBLANES]

    carry_ref[...] = lax.fori_loop(0, groups, body, carry_ref[...])


def _scan_kernel(zf_ref, zfp_ref, zfn_ref, zb_ref, zbp_ref, zbn_ref, cw_ref, cb_ref, wg_ref, bg_ref,
                 lam_ref, hf_ref, hb_ref, rc_ref, a_ref, u_ref, cf_ref, cbk_ref, *, n_ctx_tiles, n_tiles):
    i = pl.program_id(1)
    tt = rc_ref.shape[0]

    @pl.when(i == 0)
    def _():
        cf_ref[...] = jnp.zeros_like(cf_ref)
        cbk_ref[...] = jnp.zeros_like(cbk_ref)

    jb = jnp.where(i < n_ctx_tiles, n_ctx_tiles - 1 - i, n_tiles - 1 - (i - n_ctx_tiles))
    cw = cw_ref[...]
    cb = cb_ref[...]
    sp = _softplus(-lam_ref[...])

    def run(z_ref, zp_ref, zn_ref, j, direction, out_ref, carry_ref):
        first = jnp.logical_or(j == 0, j == n_ctx_tiles)
        last = jnp.logical_or(j == n_ctx_tiles - 1, j == n_tiles - 1)
        prev8 = zp_ref[0] * jnp.where(first, 0.0, 1.0)
        next8 = zn_ref[0] * jnp.where(last, 0.0, 1.0)
        body, head_out, tail_out = _conv_tile(z_ref[0], prev8, next8, cw, cb)
        rc_ref[...] = body
        rc_ref[0:SUBLANES] = head_out
        rc_ref[tt - SUBLANES:tt] = tail_out
        _gates_tile(rc_ref, a_ref, u_ref, wg_ref, bg_ref, sp[direction:direction + 1], direction)
        _scan_tile(a_ref, u_ref, out_ref, carry_ref, reverse=(direction == 1))

    run(zf_ref, zfp_ref, zfn_ref, i, 0, hf_ref, cf_ref)
    run(zb_ref, zbp_ref, zbn_ref, jb, 1, hb_ref, cbk_ref)


def _scan(z, conv_w, conv_b, wg, bg, lam, n_ctx_tiles):
    bsz, s, _ = z.shape
    r = conv_w.shape[1]
    tt = TIME_TILE
    n_tiles = s // tt
    per = tt // SUBLANES
    last8 = s // SUBLANES - 1

    def bwd_tile(i):
        return jnp.where(i < n_ctx_tiles, n_ctx_tiles - 1 - i, n_tiles - 1 - (i - n_ctx_tiles))

    main_f = pl.BlockSpec((1, tt, r), lambda b, i: (b, i, 0))
    prev_f = pl.BlockSpec((1, SUBLANES, r), lambda b, i: (b, jnp.maximum(i * per - 1, 0), 0))
    next_f = pl.BlockSpec((1, SUBLANES, r), lambda b, i: (b, jnp.minimum((i + 1) * per, last8), 0))
    main_b = pl.BlockSpec((1, tt, r), lambda b, i: (b, bwd_tile(i), 0))
    prev_b = pl.BlockSpec((1, SUBLANES, r), lambda b, i: (b, jnp.maximum(bwd_tile(i) * per - 1, 0), 0))
    next_b = pl.BlockSpec((1, SUBLANES, r), lambda b, i: (b, jnp.minimum((bwd_tile(i) + 1) * per, last8), 0))
    full = lambda shape: pl.BlockSpec(shape, lambda b, i: (0,) * len(shape))
    return pl.pallas_call(
        functools.partial(_scan_kernel, n_ctx_tiles=n_ctx_tiles, n_tiles=n_tiles),
        grid=(bsz, n_tiles),
        in_specs=[main_f, prev_f, next_f, main_b, prev_b, next_b,
                  full(conv_w.shape), full(conv_b.shape), full(wg.shape), full(bg.shape), full(lam.shape)],
        out_specs=[pl.BlockSpec((1, tt, r), lambda b, i: (b, i, 0)),
                   pl.BlockSpec((1, tt, r), lambda b, i: (b, bwd_tile(i), 0))],
        out_shape=[jax.ShapeDtypeStruct((bsz, s, r), F32), jax.ShapeDtypeStruct((bsz, s, r), F32)],
        scratch_shapes=[pltpu.VMEM((tt, r), F32), pltpu.VMEM((tt, r), F32), pltpu.VMEM((tt, r), F32),
                        pltpu.VMEM((1, r), F32), pltpu.VMEM((1, r), F32)],
        compiler_params=pltpu.CompilerParams(
            dimension_semantics=("parallel", "arbitrary"), vmem_limit_bytes=VMEM_LIMIT),
        name="lru_scan",
    )(z, z, z, z, z, z, conv_w, conv_b, wg, bg, lam)


def _finish_kernel(zg_ref, zu_ref, zv_ref, hf_ref, hb_ref, x_ref, mod_ref, glru_ref, gmlp_ref, gv_ref,
                   ws_ref, bs_ref, wout_ref, wr_ref, br_ref,
                   xo_ref, tokp_ref, route_ref, wt_ref, cnt_ref, ym_ref, run_ref, *, n_experts):
    tt, r = ym_ref.shape
    hd = r // MLP_HEADS

    @pl.when(jnp.logical_and(pl.program_id(0) == 0, pl.program_id(1) == 0))
    def _():
        run_ref[...] = jnp.zeros_like(run_ref)

    y_lru = (hf_ref[0] + hb_ref[0]) * _gelu(zg_ref[0])
    yl = _rms(y_lru) * glru_ref[...]
    u = _gelu(zu_ref[0])
    vb = (_rms(_gelu(zv_ref[0])) * gv_ref[...]).astype(BF16)
    for ch in range(tt // CHUNK):
        rows = slice(ch * CHUNK, (ch + 1) * CHUNK)
        for h in range(MLP_HEADS):
            cols = slice(h * hd, (h + 1) * hd)
            mixed = jnp.dot(ws_ref[h], vb[rows, cols], preferred_element_type=F32) + bs_ref[h]
            ym_ref[rows, cols] = u[rows, cols] * mixed
    ym = _rms(ym_ref[...]) * gmlp_ref[...]
    y = (jnp.dot(yl.astype(BF16), wout_ref[0:r], preferred_element_type=F32)
         + jnp.dot(ym.astype(BF16), wout_ref[r:2 * r], preferred_element_type=F32))
    m = mod_ref[0, 0]
    xnew = x_ref[0] + m[2:3] * y
    xo_ref[0] = xnew
    tok = _rms(xnew) * (1.0 + m[4:5]) + m[3:4]

    half = tok.shape[1] // 2
    t_hi = tok.astype(BF16)
    t_hi32 = t_hi.astype(F32)
    bits = lax.bitcast_convert_type(t_hi32, U32)
    tokp_ref[0] = bits[:, :half] | (bits[:, half:] >> 16)

    t_lo = (tok - t_hi32).astype(BF16)
    logits = (jnp.dot(t_hi, wr_ref[0], preferred_element_type=F32)
              + jnp.dot(t_lo, wr_ref[0], preferred_element_type=F32)
              + jnp.dot(t_hi, wr_ref[1], preferred_element_type=F32)) + br_ref[...]
    lane = lax.broadcasted_iota(jnp.int32, logits.shape, 1)
    neg = jnp.float32(-jnp.inf)
    work = jnp.where(lane < n_experts, logits, neg)
    vals, idxs = [], []
    for _ in range(TOP_K):
        mx = jnp.max(work, axis=-1, keepdims=True)
        ix = jnp.min(jnp.where(work == mx, lane, LANES), axis=-1, keepdims=True)
        vals.append(mx)
        idxs.append(ix)
        work = jnp.where(lane == ix, neg, work)
    exps = [jnp.exp(v - vals[0]) for v in vals]
    denom = exps[0] + exps[1] + exps[2] + exps[3]

    hot = [lane == ix for ix in idxs]
    multi = jnp.zeros(logits.shape, F32)
    for k in range(TOP_K):
        multi = multi + hot[k].astype(F32)
    before = (lax.broadcasted_iota(jnp.int32, (tt, tt), 1) < lax.broadcasted_iota(jnp.int32, (tt, tt), 0))
    excl = jnp.dot(before.astype(BF16), multi.astype(BF16), preferred_element_type=F32) + run_ref[...]
    run_ref[...] = run_ref[...] + jnp.sum(multi, axis=0, keepdims=True)
    cnt_ref[...] = jnp.broadcast_to(run_ref[...], cnt_ref.shape).astype(jnp.int32)

    route = jnp.zeros(logits.shape, jnp.int32)
    wt_out = jnp.zeros(logits.shape, F32)
    for k in range(TOP_K):
        rank = jnp.sum(jnp.where(hot[k], excl, 0.0), axis=-1, keepdims=True).astype(jnp.int32)
        route = jnp.where(lane == k, idxs[k], route)
        route = jnp.where(lane == TOP_K + k, rank, route)
        wt_out = jnp.where(lane == k, exps[k] / denom, wt_out)
    route_ref[0] = route
    wt_ref[0] = wt_out


def _finish(z, hf, hb, x, modt, g_lru, g_mlp, g_v, ws, bs, w_out_bf16, wr_split, br_pad, n_experts,
            tile_offset, n_ctx_tiles):
    bsz, s, d = x.shape
    r = hf.shape[2]
    tt = TIME_TILE
    n_out_tiles = s // tt - tile_offset
    rows_out = n_out_tiles * tt
    off = tile_offset

    def zcol(c):
        return pl.BlockSpec((1, tt, r), lambda b, i: (b, i + off, c))

    full = lambda shape: pl.BlockSpec(shape, lambda b, i: (0,) * len(shape))
    tile_in = lambda w: pl.BlockSpec((1, tt, w), lambda b, i: (b, i + off, 0))
    tile_out = lambda w: pl.BlockSpec((1, tt, w), lambda b, i: (b, i, 0))
    return pl.pallas_call(
        functools.partial(_finish_kernel, n_experts=n_experts),
        grid=(bsz, n_out_tiles),
        in_specs=[zcol(1), zcol(2), zcol(3), tile_in(r), tile_in(r), tile_in(d),
                  pl.BlockSpec((1, 1, 6, d), lambda b, i: (b, jnp.where(i + off < n_ctx_tiles, 0, 1), 0, 0)),
                  full(g_lru.shape), full(g_mlp.shape), full(g_v.shape), full(ws.shape), full(bs.shape),
                  pl.BlockSpec(w_out_bf16.shape, lambda b, i: (0, 0), pipeline_mode=pl.Buffered(1)),
                  full(wr_split.shape), full(br_pad.shape)],
        out_specs=[tile_out(d), tile_out(d // 2), tile_out(LANES), tile_out(LANES),
                   pl.BlockSpec((SUBLANES, LANES), lambda b, i: (0, 0))],
        out_shape=[jax.ShapeDtypeStruct((bsz, rows_out, d), F32),
                   jax.ShapeDtypeStruct((bsz, rows_out, d // 2), U32),
                   jax.ShapeDtypeStruct((bsz, rows_out, LANES), jnp.int32),
                   jax.ShapeDtypeStruct((bsz, rows_out, LANES), F32),
                   jax.ShapeDtypeStruct((SUBLANES, LANES), jnp.int32)],
        scratch_shapes=[pltpu.VMEM((tt, r), F32), pltpu.VMEM((1, LANES), F32)],
        compiler_params=pltpu.CompilerParams(
            dimension_semantics=("arbitrary", "arbitrary"), vmem_limit_bytes=VMEM_LIMIT),
        name="mixer_finish",
    )(z, z, z, hf, hb, x, modt, g_lru, g_mlp, g_v, ws, bs, w_out_bf16, wr_split, br_pad)


def _row_copy(src_ref, src_row, dst_ref, dst_row, sem):
    return pltpu.make_async_copy(src_ref.at[pl.ds(src_row, 1)], dst_ref.at[pl.ds(dst_row, 1)], sem)


def _wait_rows(src_ref, dst_ref, sem, n_rows):
    def group(_, carry):
        for _ in range(WAIT_GROUP):
            _row_copy(src_ref, 0, dst_ref, 0, sem).wait()
        return carry

    lax.fori_loop(0, n_rows // WAIT_GROUP, group, 0)


def _zero_pieces(tile):
    return [1 << b for b in range(tile.bit_length() - 2, 2, -1)]


def _dispatch_kernel(zstart_ref, zlen_ref, dest_ref, tok_ref, xs_ref, zbuf, sem, zsem, *, n_experts):
    s = pl.program_id(0)
    td = dest_ref.shape[2] // TOP_K
    pieces = _zero_pieces(EXPERT_TILE)

    @pl.when(s == 0)
    def _():
        zbuf[...] = jnp.zeros_like(zbuf)

        def fill(e, carry):
            gap = zlen_ref[e]
            start = zstart_ref[e]
            lead = jnp.minimum((SUBLANES - (start & (SUBLANES - 1))) & (SUBLANES - 1), gap)
            for i in range(SUBLANES - 1):
                @pl.when(i < lead)
                def _():
                    _row_copy(zbuf, 0, xs_ref, start + i, zsem).start()
            rest = gap - lead
            off = start + lead
            for p in pieces:
                @pl.when((rest & p) != 0)
                def _():
                    pltpu.make_async_copy(zbuf.at[pl.ds(0, p)],
                                          xs_ref.at[pl.ds(pl.multiple_of(off, SUBLANES), p)], zsem).start()

                off = off + (rest & p)
            for i in range(SUBLANES - 1):
                @pl.when(i < lead)
                def _():
                    _row_copy(zbuf, 0, xs_ref, 0, zsem).wait()
            for p in pieces:
                @pl.when((rest & p) != 0)
                def _():
                    pltpu.make_async_copy(zbuf.at[pl.ds(0, p)], xs_ref.at[pl.ds(0, p)], zsem).wait()
            return carry

        lax.fori_loop(0, n_experts, fill, 0)

        zrows = zbuf.shape[0]
        first_free = (zstart_ref[n_experts - 1] + zlen_ref[n_experts - 1]) // zrows
        n_chunks = xs_ref.shape[0] // zrows

        def fill_tail(c, carry):
            row = pl.multiple_of(c * zrows, zrows)
            cp = pltpu.make_async_copy(zbuf, xs_ref.at[pl.ds(row, zrows)], zsem)
            cp.start()
            cp.wait()
            return carry

        lax.fori_loop(first_free, n_chunks, fill_tail, 0)

    def issue(t, carry):
        for k in range(TOP_K):
            _row_copy(tok_ref, t, xs_ref, dest_ref[0, 0, t * TOP_K + k], sem).start()
        return carry

    lax.fori_loop(0, td, issue, 0, unroll=4)
    _wait_rows(tok_ref, xs_ref, sem, td * TOP_K)


def _dispatch(tokp, dest, zstart, zlen, n_rows):
    n_tok, dh = tokp.shape
    td = DISPATCH_TILE
    n_steps = n_tok // td
    n_experts = zstart.shape[0]
    zrows = _zero_pieces(EXPERT_TILE)[0]
    return pl.pallas_call(
        functools.partial(_dispatch_kernel, n_experts=n_experts),
        grid_spec=pltpu.PrefetchScalarGridSpec(
            num_scalar_prefetch=2,
            grid=(n_steps,),
            in_specs=[pl.BlockSpec((1, 1, td * TOP_K), lambda s, a, b: (s, 0, 0), memory_space=pltpu.SMEM),
                      pl.BlockSpec((td, dh), lambda s, a, b: (s, 0))],
            out_specs=pl.BlockSpec(memory_space=pl.ANY),
            scratch_shapes=[pltpu.VMEM((zrows, dh), U32), pltpu.SemaphoreType.DMA(()),
                            pltpu.SemaphoreType.DMA(())],
        ),
        out_shape=jax.ShapeDtypeStruct((n_rows, dh), U32),
        compiler_params=pltpu.CompilerParams(dimension_semantics=("arbitrary",)),
        name="expert_dispatch",
    )(zstart, zlen, dest.reshape(n_steps, 1, td * TOP_K), tokp)


def _cast_rows(staging_ref, slot, resident_ref):
    def body(c, carry):
        rows = pl.ds(pl.multiple_of(c * CAST_ROWS, CAST_ROWS), CAST_ROWS)
        resident_ref[rows, :] = staging_ref[slot, rows, :].astype(BF16)
        return carry

    lax.fori_loop(0, resident_ref.shape[0] // CAST_ROWS, body, 0, unroll=2)


def _stream_expert_weights(be_ref, ord_ref, nxt_ref, meta_ref, n_sweeps, start, wait, cast):
    n = pl.program_id(0)
    j = pl.program_id(1)
    e = be_ref[j]
    used = j < meta_ref[0]
    changed = jnp.logical_and(used, jnp.logical_or(j == 0, e != be_ref[jnp.maximum(j - 1, 0)]))
    slot = (n * meta_ref[1] + ord_ref[j]) % 2

    @pl.when(jnp.logical_and(n == 0, j == 0))
    def _():
        start(e, n, slot)

    @pl.when(changed)
    def _():
        wait(e, n, slot)
        cast(slot)
        following = nxt_ref[j]

        @pl.when(following >= 0)
        def _():
            start(following, n, 1 - slot)

        @pl.when(jnp.logical_and(following < 0, n + 1 < n_sweeps))
        def _():
            start(be_ref[0], n + 1, 1 - slot)

    return used


def _expert_up_kernel(be_ref, ord_ref, nxt_ref, meta_ref, fill_ref, x_ref, wgu_ref, bgate_ref, bup_ref, h_ref,
                      wbuf_g, wbuf_u, wg_s, wu_s, sem, *, layer, n_sweeps):
    tn = h_ref.shape[1]

    def copies(e, n, slot):
        gate_cols = pl.ds(pl.multiple_of(n * tn, tn), tn)
        up_cols = pl.ds(pl.multiple_of((n_sweeps + n) * tn, tn), tn)
        return (pltpu.make_async_copy(wgu_ref.at[layer, e, :, gate_cols], wbuf_g.at[slot], sem.at[0, slot]),
                pltpu.make_async_copy(wgu_ref.at[layer, e, :, up_cols], wbuf_u.at[slot], sem.at[1, slot]))

    def start(e, n, slot):
        for cp in copies(e, n, slot):
            cp.start()

    def wait(e, n, slot):
        for cp in copies(e, n, slot):
            cp.wait()

    def cast(slot):
        _cast_rows(wbuf_g, slot, wg_s)
        _cast_rows(wbuf_u, slot, wu_s)

    used = _stream_expert_weights(be_ref, ord_ref, nxt_ref, meta_ref, n_sweeps, start, wait, cast)
    tm = h_ref.shape[0]

    def compute(n_rows):
        xp = x_ref[0:n_rows, :]
        half = xp.shape[1]
        xa = lax.bitcast_convert_type(xp & jnp.uint32(HI_MASK), F32).astype(BF16)
        xb = lax.bitcast_convert_type(xp << 16, F32).astype(BF16)
        cw = tn // UP_COL_CHUNKS
        for c in range(UP_COL_CHUNKS):
            cols = slice(c * cw, (c + 1) * cw)
            gate = (jnp.dot(xa, wg_s[0:half, cols], preferred_element_type=F32)
                    + jnp.dot(xb, wg_s[half:2 * half, cols], preferred_element_type=F32)) + bgate_ref[0, 0, :, cols]
            up = (jnp.dot(xa, wu_s[0:half, cols], preferred_element_type=F32)
                  + jnp.dot(xb, wu_s[half:2 * half, cols], preferred_element_type=F32)) + bup_ref[0, 0, :, cols]
            gate = jnp.minimum(gate, SWIGLU_LIMIT)
            up = jnp.clip(up, -SWIGLU_LIMIT, SWIGLU_LIMIT)
            act = (up + 1.0) * gate * jax.nn.sigmoid(SWIGLU_ALPHA * gate)
            h_ref[0:n_rows, cols] = act.astype(BF16)

    _rows_by_fill(used, fill_ref[pl.program_id(1)], tm, compute, h_ref)


def _rows_by_fill(used, filled_rows, tm, compute, out_ref):
    wide = jnp.logical_and(used, filled_rows > tm // 2)
    narrow = jnp.logical_and(used, filled_rows <= tm // 2)

    @pl.when(wide)
    def _():
        compute(tm)

    @pl.when(narrow)
    def _():
        compute(tm // 2)
        out_ref[tm // 2:tm, :] = jnp.zeros((tm - tm // 2, out_ref.shape[1]), out_ref.dtype)

    @pl.when(jnp.logical_not(used))
    def _():
        out_ref[...] = jnp.zeros_like(out_ref)


def _expert_up(xs, w_gu, b_gu, layer, plan):
    n_rows, dh = xs.shape
    d = 2 * dh
    _, n_exp, _, two_f = w_gu.shape
    f = two_f // 2
    tm = EXPERT_TILE
    tn = UP_COLS
    n_blocks = n_rows // tm
    nf = f // tn
    b4 = b_gu.reshape(b_gu.shape[0], n_exp, 1, two_f)
    return pl.pallas_call(
        functools.partial(_expert_up_kernel, layer=layer, n_sweeps=nf),
        grid_spec=pltpu.PrefetchScalarGridSpec(
            num_scalar_prefetch=5,
            grid=(nf, n_blocks),
            in_specs=[pl.BlockSpec((tm, dh), lambda n, j, be, od, nx, mt, fl: (j, 0)),
                      pl.BlockSpec(memory_space=pl.ANY),
                      pl.BlockSpec((1, 1, 1, tn), lambda n, j, be, od, nx, mt, fl: (layer, be[j], 0, n)),
                      pl.BlockSpec((1, 1, 1, tn), lambda n, j, be, od, nx, mt, fl: (layer, be[j], 0, nf + n))],
            out_specs=pl.BlockSpec((tm, tn), lambda n, j, be, od, nx, mt, fl: (j, n)),
            scratch_shapes=[pltpu.VMEM((2, d, tn), F32), pltpu.VMEM((2, d, tn), F32),
                            pltpu.VMEM((d, tn), BF16), pltpu.VMEM((d, tn), BF16),
                            pltpu.SemaphoreType.DMA((2, 2))],
        ),
        out_shape=jax.ShapeDtypeStruct((n_rows, f), BF16),
        compiler_params=pltpu.CompilerParams(
            dimension_semantics=("arbitrary", "arbitrary"), vmem_limit_bytes=EXPERT_VMEM_LIMIT),
        name="expert_up",
    )(*plan, xs, w_gu, b4, b4)


def _expert_down_kernel(be_ref, ord_ref, nxt_ref, meta_ref, fill_ref, h_ref, wd_ref, bd_ref, o_ref, wbuf, wd_s,
                        sem, *, layer):
    def copy(e, slot):
        return pltpu.make_async_copy(wd_ref.at[layer, e], wbuf.at[slot], sem.at[slot])

    def cast(slot):
        _cast_rows(wbuf, slot, wd_s)

    used = _stream_expert_weights(be_ref, ord_ref, nxt_ref, meta_ref, 1,
                                  lambda e, n, slot: copy(e, slot).start(),
                                  lambda e, n, slot: copy(e, slot).wait(), cast)

    def compute(n_rows):
        hb = h_ref[0:n_rows, :]
        half = o_ref.shape[1]
        cw = half // DOWN_COL_CHUNKS
        for c in range(DOWN_COL_CHUNKS):
            lo = slice(c * cw, (c + 1) * cw)
            hi = slice(half + c * cw, half + (c + 1) * cw)
            ya = jnp.dot(hb, wd_s[:, lo], preferred_element_type=F32) + bd_ref[0, 0, :, lo]
            yb = jnp.dot(hb, wd_s[:, hi], preferred_element_type=F32) + bd_ref[0, 0, :, hi]
            ba = lax.bitcast_convert_type(ya.astype(BF16).astype(F32), U32)
            bb = lax.bitcast_convert_type(yb.astype(BF16).astype(F32), U32)
            o_ref[0:n_rows, lo] = ba | (bb >> 16)

    _rows_by_fill(used, fill_ref[pl.program_id(1)], o_ref.shape[0], compute, o_ref)


def _expert_down(h, w_down, b_down, layer, plan):
    n_rows, f = h.shape
    _, n_exp, _, d = w_down.shape
    tm = EXPERT_TILE
    n_blocks = n_rows // tm
    return pl.pallas_call(
        functools.partial(_expert_down_kernel, layer=layer),
        grid_spec=pltpu.PrefetchScalarGridSpec(
            num_scalar_prefetch=5,
            grid=(1, n_blocks),
            in_specs=[pl.BlockSpec((tm, f), lambda n, j, be, od, nx, mt, fl: (j, 0)),
                      pl.BlockSpec(memory_space=pl.ANY),
                      pl.BlockSpec((1, 1, 1, d), lambda n, j, be, od, nx, mt, fl: (layer, be[j], 0, 0))],
            out_specs=pl.BlockSpec((tm, d // 2), lambda n, j, be, od, nx, mt, fl: (j, 0)),
            scratch_shapes=[pltpu.VMEM((2, f, d), F32), pltpu.VMEM((f, d), BF16), pltpu.SemaphoreType.DMA((2,))],
        ),
        out_shape=jax.ShapeDtypeStruct((n_rows, d // 2), U32),
        compiler_params=pltpu.CompilerParams(
            dimension_semantics=("arbitrary", "arbitrary"), vmem_limit_bytes=EXPERT_VMEM_LIMIT),
        name="expert_down",
    )(*plan, h, w_down, b_down.reshape(b_down.shape[0], n_exp, 1, d))


def _combine_kernel(destc_ref, destn_ref, outs_ref, w_ref, x_ref, mod_ref, gfin_ref, o_ref, buf, sem, *, final):
    t = pl.program_id(0)
    nt = pl.num_programs(0)
    tb = x_ref.shape[0]

    def issue(dest_ref, slot):
        def body(r, carry):
            for k in range(TOP_K):
                p = dest_ref[0, 0, r * TOP_K + k]
                pltpu.make_async_copy(outs_ref.at[pl.ds(p, 1)], buf.at[slot, k, pl.ds(r, 1)], sem.at[slot]).start()
            return carry

        lax.fori_loop(0, tb, body, 0, unroll=4)

    @pl.when(t == 0)
    def _():
        issue(destc_ref, 0)

    @pl.when(t + 1 < nt)
    def _():
        issue(destn_ref, (t + 1) % 2)

    slot = t % 2

    def group(_, carry):
        for _ in range(WAIT_GROUP):
            pltpu.make_async_copy(outs_ref.at[pl.ds(0, 1)], buf.at[slot, 0, pl.ds(0, 1)], sem.at[slot]).wait()
        return carry

    lax.fori_loop(0, tb * TOP_K // WAIT_GROUP, group, 0)

    w = w_ref[...]
    half = buf.shape[3]
    ya = jnp.zeros((tb, half), F32)
    yb = jnp.zeros((tb, half), F32)
    for k in range(TOP_K):
        p = buf[slot, k]
        wk = w[:, k:k + 1]
        ya = ya + wk * lax.bitcast_convert_type(p & jnp.uint32(HI_MASK), F32)
        yb = yb + wk * lax.bitcast_convert_type(p << 16, F32)
    m = mod_ref[0, 0]
    xa = x_ref[:, 0:half] + m[5:6, 0:half] * ya
    xb = x_ref[:, half:2 * half] + m[5:6, half:2 * half] * yb
    if final:
        ms = (jnp.sum(xa * xa, axis=-1, keepdims=True) + jnp.sum(xb * xb, axis=-1, keepdims=True)) / (2 * half)
        scale = lax.rsqrt(ms + EPS)
        xa = xa * scale * gfin_ref[:, 0:half]
        xb = xb * scale * gfin_ref[:, half:2 * half]
    o_ref[:, 0:half] = xa
    o_ref[:, half:2 * half] = xb


def _combine(outs, dest, wt, x_flat, modt, g_final, tiles_per_batch, n_ctx_tiles, final):
    n_tok, d = x_flat.shape
    tb = COMBINE_TILE
    n_tiles = n_tok // tb
    cur = lambda t: (t, 0, 0)
    nxt = lambda t: (jnp.minimum(t + 1, n_tiles - 1), 0, 0)
    smem = lambda imap: pl.BlockSpec((1, 1, tb * TOP_K), imap, memory_space=pltpu.SMEM)
    dest3 = dest.reshape(n_tiles, 1, tb * TOP_K)
    return pl.pallas_call(
        functools.partial(_combine_kernel, final=final),
        grid=(n_tiles,),
        in_specs=[smem(cur), smem(nxt),
                  pl.BlockSpec(memory_space=pl.ANY),
                  pl.BlockSpec((tb, LANES), lambda t: (t, 0)),
                  pl.BlockSpec((tb, d), lambda t: (t, 0)),
                  pl.BlockSpec((1, 1, 6, d), lambda t: (
                      t // tiles_per_batch, jnp.where(t % tiles_per_batch < n_ctx_tiles, 0, 1), 0, 0)),
                  pl.BlockSpec((1, d), lambda t: (0, 0))],
        out_specs=pl.BlockSpec((tb, d), lambda t: (t, 0)),
        out_shape=jax.ShapeDtypeStruct((n_tok, d), F32),
        scratch_shapes=[pltpu.VMEM((2, TOP_K, tb, d // 2), U32), pltpu.SemaphoreType.DMA((2,))],
        compiler_params=pltpu.CompilerParams(
            dimension_semantics=("arbitrary",), vmem_limit_bytes=VMEM_LIMIT),
        name="expert_combine",
    )(dest3, dest3, outs, wt, x_flat, modt, g_final.reshape(1, d))


def _expert_layout(counts, n_assign):
    n_experts = counts.shape[0]
    tm = EXPERT_TILE
    padded = (counts + tm - 1) // tm * tm
    ends_p = jnp.cumsum(padded).astype(jnp.int32)
    starts_p = ends_p - padded
    n_blocks = (n_assign + n_experts * (tm - 1) + tm - 1) // tm
    block_start = jnp.arange(n_blocks, dtype=jnp.int32) * tm
    block_e = jnp.minimum(jnp.sum((ends_p[None, :] <= block_start[:, None]).astype(jnp.int32), axis=1),
                          n_experts - 1).astype(jnp.int32)
    n_used = ends_p[-1] // tm
    tile = jnp.arange(n_blocks, dtype=jnp.int32)
    used = tile < n_used
    first = jnp.logical_and(used, jnp.concatenate([jnp.ones((1,), bool), block_e[1:] != block_e[:-1]]))
    ordinal = (jnp.cumsum(first.astype(jnp.int32)) - 1).astype(jnp.int32)
    first_pos = jnp.where(first, tile, n_blocks)
    after = jnp.concatenate([first_pos[1:], jnp.full((1,), n_blocks, jnp.int32)])
    next_first = lax.cummin(after, axis=0, reverse=True)
    following = jnp.where(next_first < n_blocks, block_e[jnp.minimum(next_first, n_blocks - 1)], -1).astype(jnp.int32)
    meta = jnp.stack([n_used, jnp.sum(first.astype(jnp.int32))]).astype(jnp.int32)
    filled = jnp.clip((starts_p + counts)[block_e] - tile * tm, 0, tm).astype(jnp.int32)
    plan = (block_e, ordinal, following, meta, filled)
    return starts_p, starts_p + counts, padded - counts, plan, n_blocks * tm


def _moe(tokp, route, counts, w_gu, b_gu, w_down, b_down, layer):
    n_tok = tokp.shape[0]
    n_experts = counts.shape[0]
    idx4 = route[:, :TOP_K]
    rank4 = route[:, TOP_K:2 * TOP_K]
    starts_p, zstart, zlen, plan, n_rows = _expert_layout(counts, n_tok * TOP_K)
    onehot = idx4[:, :, None] == jnp.arange(n_experts, dtype=jnp.int32)[None, None, :]
    dest = (jnp.sum(jnp.where(onehot, starts_p[None, None, :], 0), axis=-1) + rank4).astype(jnp.int32)
    xs = _dispatch(tokp, dest, zstart, zlen, n_rows)
    h = _expert_up(xs, w_gu, b_gu, layer, plan)
    outs = _expert_down(h, w_down, b_down, layer, plan)
    return outs, dest


def kernel(x, c, ctx, c_ctx, w_mod, b_mod, w_in, conv_w, conv_b, w_r, b_r, w_i, b_i, lam, g_v, w_s, b_s,
           g_lru, g_mlp, w_out, w_router, b_router, w_gu, b_gu, w_down, b_down, g_final):
    bsz, n_lat, d = x.shape
    n_ctx = ctx.shape[1]
    depth = w_mod.shape[0]
    n_experts = w_router.shape[2]
    r = conv_w.shape[2]
    tt = TIME_TILE
    assert n_ctx % tt == 0 and n_lat % tt == 0 and tt % CHUNK == 0
    assert tt % DISPATCH_TILE == 0 and tt % COMBINE_TILE == 0
    assert n_experts <= LANES and r % LRU_HEADS == 0
    n_ctx_tiles = n_ctx // tt

    rows = (bsz + 1 + SUBLANES - 1) // SUBLANES * SUBLANES
    c_all = jnp.zeros((rows, d), F32).at[:bsz].set(c).at[bsz].set(c_ctx)
    mod = _modulation(c_all, w_mod, b_mod)

    hcat = jnp.concatenate([ctx, x], axis=1)
    out = None
    for l in range(depth):
        last = l == depth - 1
        mod_x = mod[l, :bsz].reshape(bsz, 1, 6, d)
        mod_c = jnp.broadcast_to(mod[l, bsz].reshape(1, 1, 6, d), (bsz, 1, 6, d))
        modt = jnp.concatenate([mod_c, mod_x], axis=1)

        z = _inproj(hcat, modt, w_in[l].astype(BF16), n_ctx_tiles)
        hd = r // LRU_HEADS
        wg = jnp.concatenate([w_r[l], w_i[l]], axis=-1).astype(BF16)
        bg = jnp.concatenate([b_r[l], b_i[l]], axis=-1).reshape(2, LRU_HEADS, 1, 2 * hd)
        hf, hb = _scan(z, conv_w[l], conv_b[l].reshape(1, r), wg, bg, lam[l], n_ctx_tiles)

        wr_hi = w_router[l].astype(BF16)
        wr_lo = (w_router[l] - wr_hi.astype(F32)).astype(BF16)
        wr_split = jnp.pad(jnp.stack([wr_hi, wr_lo]), ((0, 0), (0, 0), (0, LANES - n_experts)))
        br_pad = jnp.pad(b_router[l], (0, LANES - n_experts)).reshape(1, LANES)
        off = n_ctx_tiles if last else 0
        xo, tokp, route, wt, cnt = _finish(
            z, hf, hb, hcat, modt, g_lru[l].reshape(1, r), g_mlp[l].reshape(1, r), g_v[l].reshape(1, r),
            w_s[l].astype(BF16), b_s[l].reshape(MLP_HEADS, CHUNK, 1), w_out[l].astype(BF16),
            wr_split, br_pad, n_experts, off, n_ctx_tiles)

        rows_out = xo.shape[1]
        n_tok = bsz * rows_out
        outs, dest = _moe(
            tokp.reshape(n_tok, d // 2), route.reshape(n_tok, LANES), cnt[0, :n_experts],
            w_gu, b_gu, w_down, b_down, l)
        new = _combine(outs, dest, wt.reshape(n_tok, LANES), xo.reshape(n_tok, d), modt,
                       g_final, rows_out // COMBINE_TILE, 0 if last else n_ctx // COMBINE_TILE, last)
        new = new.reshape(bsz, rows_out, d)
        if last:
            out = new
        else:
            hcat = new
    return out
```

```python
import functools
import math

import jax
import jax.numpy as jnp
from jax import lax
from jax.experimental import pallas as pl
from jax.experimental.pallas import tpu as pltpu

LRU_HEADS = 8
MLP_HEADS = 8
CONV_W = 4
CONV_LEFT = 2
RG_C = 8.0
CHUNK = 128
TOP_K = 4
SWIGLU_LIMIT = 7.0
SWIGLU_ALPHA = 1.702
EPS = 1e-6

SUBLANES = 8
LANES = 128
TIME_TILE = 256
EXPERT_TILE = 512
DISPATCH_TILE = 128
COMBINE_TILE = 128
WAIT_GROUP = 64
DMA_QUEUES = 2
UP_COLS = 1024
UP_COL_CHUNKS = 2
DOWN_COL_CHUNKS = 2
CAST_ROWS = 32
VMEM_LIMIT = 56 * 1024 * 1024
EXPERT_VMEM_LIMIT = 60 * 1024 * 1024

F32 = jnp.float32
BF16 = jnp.bfloat16
U32 = jnp.uint32
HI_MASK = 0xFFFF0000


def _rms(x):
    return x * lax.rsqrt(jnp.mean(x * x, axis=-1, keepdims=True) + EPS)


def _gelu(x):
    c = math.sqrt(2.0 / math.pi)
    return 0.5 * x * (1.0 + jnp.tanh(c * (x + 0.044715 * (x * x * x))))


def _softplus(x):
    return jnp.maximum(x, 0.0) + jnp.log1p(jnp.exp(-jnp.abs(x)))


def _mod_kernel(c_ref, w_ref, b_ref, o_ref):
    c = c_ref[...]
    s = c * jax.nn.sigmoid(c)
    o_ref[0] = jnp.dot(s, w_ref[0], preferred_element_type=F32) + b_ref[0]


def _modulation(c_all, w_mod, b_mod):
    depth, d, n6 = w_mod.shape
    rows = c_all.shape[0]
    tn = n6 // 8
    return pl.pallas_call(
        _mod_kernel,
        grid=(depth, n6 // tn),
        in_specs=[
            pl.BlockSpec((rows, d), lambda l, n: (0, 0)),
            pl.BlockSpec((1, d, tn), lambda l, n: (l, 0, n)),
            pl.BlockSpec((1, 1, tn), lambda l, n: (l, 0, n)),
        ],
        out_specs=pl.BlockSpec((1, rows, tn), lambda l, n: (l, 0, n)),
        out_shape=jax.ShapeDtypeStruct((depth, rows, n6), F32),
        compiler_params=pltpu.CompilerParams(
            dimension_semantics=("arbitrary", "arbitrary"), vmem_limit_bytes=VMEM_LIMIT),
        name="modulation",
    )(c_all, w_mod, b_mod.reshape(depth, 1, n6))


def _inproj_kernel(x_ref, mod_ref, w_ref, z_ref, *, n_chunks):
    x = x_ref[0]
    m = mod_ref[0, 0]
    nx = _rms(x) * (1.0 + m[1:2]) + m[0:1]
    nb = nx.astype(BF16)
    cw = w_ref.shape[1] // n_chunks
    for c in range(n_chunks):
        z_ref[0, :, c * cw:(c + 1) * cw] = jnp.dot(
            nb, w_ref[:, c * cw:(c + 1) * cw], preferred_element_type=F32)


def _inproj(x, modt, w_in_bf16, n_ctx_tiles):
    bsz, s, d = x.shape
    n_cols = w_in_bf16.shape[1]
    tt = TIME_TILE
    return pl.pallas_call(
        functools.partial(_inproj_kernel, n_chunks=4),
        grid=(bsz, s // tt),
        in_specs=[
            pl.BlockSpec((1, tt, d), lambda b, i: (b, i, 0)),
            pl.BlockSpec((1, 1, 6, d), lambda b, i: (b, jnp.where(i < n_ctx_tiles, 0, 1), 0, 0)),
            pl.BlockSpec((d, n_cols), lambda b, i: (0, 0), pipeline_mode=pl.Buffered(1)),
        ],
        out_specs=pl.BlockSpec((1, tt, n_cols), lambda b, i: (b, i, 0)),
        out_shape=jax.ShapeDtypeStruct((bsz, s, n_cols), F32),
        compiler_params=pltpu.CompilerParams(
            dimension_semantics=("parallel", "arbitrary"), vmem_limit_bytes=VMEM_LIMIT),
        name="inproj",
    )(x, modt, w_in_bf16)


def _conv_tile(main, prev8, next8, cw, cb):
    tt = main.shape[0]
    row8 = lax.broadcasted_iota(jnp.int32, (SUBLANES, main.shape[1]), 0)
    w0, w1, w2, w3 = cw[0:1], cw[1:2], cw[2:3], cw[3:4]
    r1 = pltpu.roll(main, 1, 0)
    r2 = pltpu.roll(main, 2, 0)
    rm1 = pltpu.roll(main, tt - 1, 0)
    body = cb + w0 * r2 + w1 * r1 + w2 * main + w3 * rm1
    head = main[0:SUBLANES]
    h1 = jnp.where(row8 < 1, pltpu.roll(prev8, 1, 0), pltpu.roll(head, 1, 0))
    h2 = jnp.where(row8 < 2, pltpu.roll(prev8, 2, 0), pltpu.roll(head, 2, 0))
    head_out = cb + w0 * h2 + w1 * h1 + w2 * head + w3 * rm1[0:SUBLANES]
    tail_m1 = jnp.where(row8 == SUBLANES - 1, pltpu.roll(next8, SUBLANES - 1, 0), rm1[tt - SUBLANES:tt])
    tail_out = (cb + w0 * r2[tt - SUBLANES:tt] + w1 * r1[tt - SUBLANES:tt]
                + w2 * main[tt - SUBLANES:tt] + w3 * tail_m1)
    return body, head_out, tail_out


def _gates_tile(rc_ref, a_ref, u_ref, wg_ref, bg_ref, sp, direction):
    hd = rc_ref.shape[1] // LRU_HEADS
    for h in range(LRU_HEADS):
        cols = slice(h * hd, (h + 1) * hd)
        xh = rc_ref[:, cols]
        pre = jnp.dot(xh.astype(BF16), wg_ref[direction, h], preferred_element_type=F32) + bg_ref[direction, h]
        r = jax.nn.sigmoid(pre[:, :hd])
        gi = jax.nn.sigmoid(pre[:, hd:])
        a = jnp.exp((-RG_C) * r * sp[:, cols])
        a_ref[:, cols] = a
        u_ref[:, cols] = jnp.sqrt(1.0 - a * a) * gi * xh


def _scan_tile(a_ref, u_ref, out_ref, carry_ref, reverse):
    tt, width = a_ref.shape
    groups = tt // SUBLANES
    row8 = lax.broadcasted_iota(jnp.int32, (SUBLANES, width), 0)

    def body(g, carry):
        gi = (groups - 1 - g) if reverse else g
        sl = pl.ds(pl.multiple_of(gi * SUBLANES, SUBLANES), SUBLANES)
        a = a_ref[sl, :]
        u = u_ref[sl, :]
        for k in (1, 2, 4):
            if reverse:
                shift = SUBLANES - k
                valid = row8 < SUBLANES - k
            else:
                shift = k
                valid = row8 >= k
            a_s = pltpu.roll(a, shift, 0)
            u_s = pltpu.roll(u, shift, 0)
            u = jnp.where(valid, a * u_s + u, u)
            a = jnp.where(valid, a * a_s, a)
        h = a * carry + u
        out_ref[0, sl, :] = h
        return h[0:1] if reverse else h[SUBLANES - 1:SUBLANES]

    carry_ref[...] = lax.fori_loop(0, groups, body, carry_ref[...])


def _scan_kernel(zf_ref, zfp_ref, zfn_ref, zb_ref, zbp_ref, zbn_ref, cw_ref, cb_ref, wg_ref, bg_ref,
                 lam_ref, hf_ref, hb_ref, rc_ref, a_ref, u_ref, cf_ref, cbk_ref, *, n_ctx_tiles, n_tiles):
    i = pl.program_id(1)
    tt = rc_ref.shape[0]

    @pl.when(i == 0)
    def _():
        cf_ref[...] = jnp.zeros_like(cf_ref)
        cbk_ref[...] = jnp.zeros_like(cbk_ref)

    jb = jnp.where(i < n_ctx_tiles, n_ctx_tiles - 1 - i, n_tiles - 1 - (i - n_ctx_tiles))
    cw = cw_ref[...]
    cb = cb_ref[...]
    sp = _softplus(-lam_ref[...])

    def run(z_ref, zp_ref, zn_ref, j, direction, out_ref, carry_ref):
        first = jnp.logical_or(j == 0, j == n_ctx_tiles)
        last = jnp.logical_or(j == n_ctx_tiles - 1, j == n_tiles - 1)
        prev8 = zp_ref[0] * jnp.where(first, 0.0, 1.0)
        next8 = zn_ref[0] * jnp.where(last, 0.0, 1.0)
        body, head_out, tail_out = _conv_tile(z_ref[0], prev8, next8, cw, cb)
        rc_ref[...] = body
        rc_ref[0:SUBLANES] = head_out
        rc_ref[tt - SUBLANES:tt] = tail_out
        _gates_tile(rc_ref, a_ref, u_ref, wg_ref, bg_ref, sp[direction:direction + 1], direction)
        _scan_tile(a_ref, u_ref, out_ref, carry_ref, reverse=(direction == 1))

    run(zf_ref, zfp_ref, zfn_ref, i, 0, hf_ref, cf_ref)
    run(zb_ref, zbp_ref, zbn_ref, jb, 1, hb_ref, cbk_ref)


def _scan(z, conv_w, conv_b, wg, bg, lam, n_ctx_tiles):
    bsz, s, _ = z.shape
    r = conv_w.shape[1]
    tt = TIME_TILE
    n_tiles = s // tt
    per = tt // SUBLANES
    last8 = s // SUBLANES - 1

    def bwd_tile(i):
        return jnp.where(i < n_ctx_tiles, n_ctx_tiles - 1 - i, n_tiles - 1 - (i - n_ctx_tiles))

    main_f = pl.BlockSpec((1, tt, r), lambda b, i: (b, i, 0))
    prev_f = pl.BlockSpec((1, SUBLANES, r), lambda b, i: (b, jnp.maximum(i * per - 1, 0), 0))
    next_f = pl.BlockSpec((1, SUBLANES, r), lambda b, i: (b, jnp.minimum((i + 1) * per, last8), 0))
    main_b = pl.BlockSpec((1, tt, r), lambda b, i: (b, bwd_tile(i), 0))
    prev_b = pl.BlockSpec((1, SUBLANES, r), lambda b, i: (b, jnp.maximum(bwd_tile(i) * per - 1, 0), 0))
    next_b = pl.BlockSpec((1, SUBLANES, r), lambda b, i: (b, jnp.minimum((bwd_tile(i) + 1) * per, last8), 0))
    full = lambda shape: pl.BlockSpec(shape, lambda b, i: (0,) * len(shape))
    return pl.pallas_call(
        functools.partial(_scan_kernel, n_ctx_tiles=n_ctx_tiles, n_tiles=n_tiles),
        grid=(bsz, n_tiles),
        in_specs=[main_f, prev_f, next_f, main_b, prev_b, next_b,
                  full(conv_w.shape), full(conv_b.shape), full(wg.shape), full(bg.shape), full(lam.shape)],
        out_specs=[pl.BlockSpec((1, tt, r), lambda b, i: (b, i, 0)),
                   pl.BlockSpec((1, tt, r), lambda b, i: (b, bwd_tile(i), 0))],
        out_shape=[jax.ShapeDtypeStruct((bsz, s, r), F32), jax.ShapeDtypeStruct((bsz, s, r), F32)],
        scratch_shapes=[pltpu.VMEM((tt, r), F32), pltpu.VMEM((tt, r), F32), pltpu.VMEM((tt, r), F32),
                        pltpu.VMEM((1, r), F32), pltpu.VMEM((1, r), F32)],
        compiler_params=pltpu.CompilerParams(
            dimension_semantics=("parallel", "arbitrary"), vmem_limit_bytes=VMEM_LIMIT),
        name="lru_scan",
    )(z, z, z, z, z, z, conv_w, conv_b, wg, bg, lam)


def _finish_kernel(zg_ref, zu_ref, zv_ref, hf_ref, hb_ref, x_ref, mod_ref, glru_ref, gmlp_ref, gv_ref,
                   ws_ref, bs_ref, wout_ref, wr_ref, br_ref,
                   xo_ref, tokp_ref, route_ref, wt_ref, cnt_ref, ym_ref, run_ref, *, n_experts):
    tt, r = ym_ref.shape
    hd = r // MLP_HEADS

    @pl.when(jnp.logical_and(pl.program_id(0) == 0, pl.program_id(1) == 0))
    def _():
        run_ref[...] = jnp.zeros_like(run_ref)

    y_lru = (hf_ref[0] + hb_ref[0]) * _gelu(zg_ref[0])
    yl = _rms(y_lru) * glru_ref[...]
    u = _gelu(zu_ref[0])
    vb = (_rms(_gelu(zv_ref[0])) * gv_ref[...]).astype(BF16)
    for ch in range(tt // CHUNK):
        rows = slice(ch * CHUNK, (ch + 1) * CHUNK)
        for h in range(MLP_HEADS):
            cols = slice(h * hd, (h + 1) * hd)
            mixed = jnp.dot(ws_ref[h], vb[rows, cols], preferred_element_type=F32) + bs_ref[h]
            ym_ref[rows, cols] = u[rows, cols] * mixed
    ym = _rms(ym_ref[...]) * gmlp_ref[...]
    y = (jnp.dot(yl.astype(BF16), wout_ref[0:r], preferred_element_type=F32)
         + jnp.dot(ym.astype(BF16), wout_ref[r:2 * r], preferred_element_type=F32))
    m = mod_ref[0, 0]
    xnew = x_ref[0] + m[2:3] * y
    xo_ref[0] = xnew
    tok = _rms(xnew) * (1.0 + m[4:5]) + m[3:4]

    half = tok.shape[1] // 2
    t_hi = tok.astype(BF16)
    t_hi32 = t_hi.astype(F32)
    bits = lax.bitcast_convert_type(t_hi32, U32)
    tokp_ref[0] = bits[:, :half] | (bits[:, half:] >> 16)

    t_lo = (tok - t_hi32).astype(BF16)
    logits = (jnp.dot(t_hi, wr_ref[0], preferred_element_type=F32)
              + jnp.dot(t_lo, wr_ref[0], preferred_element_type=F32)
              + jnp.dot(t_hi, wr_ref[1], preferred_element_type=F32)) + br_ref[...]
    lane = lax.broadcasted_iota(jnp.int32, logits.shape, 1)
    neg = jnp.float32(-jnp.inf)
    work = jnp.where(lane < n_experts, logits, neg)
    vals, idxs = [], []
    for _ in range(TOP_K):
        mx = jnp.max(work, axis=-1, keepdims=True)
        ix = jnp.min(jnp.where(work == mx, lane, LANES), axis=-1, keepdims=True)
        vals.append(mx)
        idxs.append(ix)
        work = jnp.where(lane == ix, neg, work)
    exps = [jnp.exp(v - vals[0]) for v in vals]
    denom = exps[0] + exps[1] + exps[2] + exps[3]

    hot = [lane == ix for ix in idxs]
    multi = jnp.zeros(logits.shape, F32)
    for k in range(TOP_K):
        multi = multi + hot[k].astype(F32)
    before = (lax.broadcasted_iota(jnp.int32, (tt, tt), 1) < lax.broadcasted_iota(jnp.int32, (tt, tt), 0))
    excl = jnp.dot(before.astype(BF16), multi.astype(BF16), preferred_element_type=F32) + run_ref[...]
    run_ref[...] = run_ref[...] + jnp.sum(multi, axis=0, keepdims=True)
    cnt_ref[...] = jnp.broadcast_to(run_ref[...], cnt_ref.shape).astype(jnp.int32)

    route = jnp.zeros(logits.shape, jnp.int32)
    wt_out = jnp.zeros(logits.shape, F32)
    for k in range(TOP_K):
        rank = jnp.sum(jnp.where(hot[k], excl, 0.0), axis=-1, keepdims=True).astype(jnp.int32)
        route = jnp.where(lane == k, idxs[k], route)
        route = jnp.where(lane == TOP_K + k, rank, route)
        wt_out = jnp.where(lane == k, exps[k] / denom, wt_out)
    route_ref[0] = route
    wt_ref[0] = wt_out


def _finish(z, hf, hb, x, modt, g_lru, g_mlp, g_v, ws, bs, w_out_bf16, wr_split, br_pad, n_experts,
            tile_offset, n_ctx_tiles):
    bsz, s, d = x.shape
    r = hf.shape[2]
    tt = TIME_TILE
    n_out_tiles = s // tt - tile_offset
    rows_out = n_out_tiles * tt
    off = tile_offset

    def zcol(c):
        return pl.BlockSpec((1, tt, r), lambda b, i: (b, i + off, c))

    full = lambda shape: pl.BlockSpec(shape, lambda b, i: (0,) * len(shape))
    tile_in = lambda w: pl.BlockSpec((1, tt, w), lambda b, i: (b, i + off, 0))
    tile_out = lambda w: pl.BlockSpec((1, tt, w), lambda b, i: (b, i, 0))
    return pl.pallas_call(
        functools.partial(_finish_kernel, n_experts=n_experts),
        grid=(bsz, n_out_tiles),
        in_specs=[zcol(1), zcol(2), zcol(3), tile_in(r), tile_in(r), tile_in(d),
                  pl.BlockSpec((1, 1, 6, d), lambda b, i: (b, jnp.where(i + off < n_ctx_tiles, 0, 1), 0, 0)),
                  full(g_lru.shape), full(g_mlp.shape), full(g_v.shape), full(ws.shape), full(bs.shape),
                  pl.BlockSpec(w_out_bf16.shape, lambda b, i: (0, 0), pipeline_mode=pl.Buffered(1)),
                  full(wr_split.shape), full(br_pad.shape)],
        out_specs=[tile_out(d), tile_out(d // 2), tile_out(LANES), tile_out(LANES),
                   pl.BlockSpec((SUBLANES, LANES), lambda b, i: (0, 0))],
        out_shape=[jax.ShapeDtypeStruct((bsz, rows_out, d), F32),
                   jax.ShapeDtypeStruct((bsz, rows_out, d // 2), U32),
                   jax.ShapeDtypeStruct((bsz, rows_out, LANES), jnp.int32),
                   jax.ShapeDtypeStruct((bsz, rows_out, LANES), F32),
                   jax.ShapeDtypeStruct((SUBLANES, LANES), jnp.int32)],
        scratch_shapes=[pltpu.VMEM((tt, r), F32), pltpu.VMEM((1, LANES), F32)],
        compiler_params=pltpu.CompilerParams(
            dimension_semantics=("arbitrary", "arbitrary"), vmem_limit_bytes=VMEM_LIMIT),
        name="mixer_finish",
    )(z, z, z, hf, hb, x, modt, g_lru, g_mlp, g_v, ws, bs, w_out_bf16, wr_split, br_pad)


def _row_copy(src_ref, src_row, dst_ref, dst_row, sem):
    return pltpu.make_async_copy(src_ref.at[pl.ds(src_row, 1)], dst_ref.at[pl.ds(dst_row, 1)], sem)


def _wait_rows(src_ref, dst_ref, sem, n_rows):
    def group(_, carry):
        for _ in range(WAIT_GROUP):
            _row_copy(src_ref, 0, dst_ref, 0, sem).wait()
        return carry

    lax.fori_loop(0, n_rows // WAIT_GROUP, group, 0)


def _zero_pieces(tile):
    return [1 << b for b in range(tile.bit_length() - 2, 2, -1)]


def _dispatch_kernel(zstart_ref, zlen_ref, dest_ref, tok_ref, xs_ref, zbuf, sem, zsem, *, n_experts):
    s = pl.program_id(0)
    td = dest_ref.shape[2] // TOP_K
    pieces = _zero_pieces(EXPERT_TILE)

    @pl.when(s == 0)
    def _():
        zbuf[...] = jnp.zeros_like(zbuf)

        def fill(e, carry):
            gap = zlen_ref[e]
            start = zstart_ref[e]
            lead = jnp.minimum((SUBLANES - (start & (SUBLANES - 1))) & (SUBLANES - 1), gap)
            for i in range(SUBLANES - 1):
                @pl.when(i < lead)
                def _():
                    _row_copy(zbuf, 0, xs_ref, start + i, zsem).start()
            rest = gap - lead
            off = start + lead
            for p in pieces:
                @pl.when((rest & p) != 0)
                def _():
                    pltpu.make_async_copy(zbuf.at[pl.ds(0, p)],
                                          xs_ref.at[pl.ds(pl.multiple_of(off, SUBLANES), p)], zsem).start()

                off = off + (rest & p)
            for i in range(SUBLANES - 1):
                @pl.when(i < lead)
                def _():
                    _row_copy(zbuf, 0, xs_ref, 0, zsem).wait()
            for p in pieces:
                @pl.when((rest & p) != 0)
                def _():
                    pltpu.make_async_copy(zbuf.at[pl.ds(0, p)], xs_ref.at[pl.ds(0, p)], zsem).wait()
            return carry

        lax.fori_loop(0, n_experts, fill, 0)

        zrows = zbuf.shape[0]
        first_free = (zstart_ref[n_experts - 1] + zlen_ref[n_experts - 1]) // zrows
        n_chunks = xs_ref.shape[0] // zrows

        def fill_tail(c, carry):
            row = pl.multiple_of(c * zrows, zrows)
            cp = pltpu.make_async_copy(zbuf, xs_ref.at[pl.ds(row, zrows)], zsem)
            cp.start()
            cp.wait()
            return carry

        lax.fori_loop(first_free, n_chunks, fill_tail, 0)

    def issue(t, carry):
        for k in range(TOP_K):
            _row_copy(tok_ref, t, xs_ref, dest_ref[0, 0, t * TOP_K + k], sem).start(priority=k % DMA_QUEUES)
        return carry

    lax.fori_loop(0, td, issue, 0, unroll=4)
    _wait_rows(tok_ref, xs_ref, sem, td * TOP_K)


def _dispatch(tokp, dest, zstart, zlen, n_rows):
    n_tok, dh = tokp.shape
    td = DISPATCH_TILE
    n_steps = n_tok // td
    n_experts = zstart.shape[0]
    zrows = _zero_pieces(EXPERT_TILE)[0]
    return pl.pallas_call(
        functools.partial(_dispatch_kernel, n_experts=n_experts),
        grid_spec=pltpu.PrefetchScalarGridSpec(
            num_scalar_prefetch=2,
            grid=(n_steps,),
            in_specs=[pl.BlockSpec((1, 1, td * TOP_K), lambda s, a, b: (s, 0, 0), memory_space=pltpu.SMEM),
                      pl.BlockSpec((td, dh), lambda s, a, b: (s, 0))],
            out_specs=pl.BlockSpec(memory_space=pl.ANY),
            scratch_shapes=[pltpu.VMEM((zrows, dh), U32), pltpu.SemaphoreType.DMA(()),
                            pltpu.SemaphoreType.DMA(())],
        ),
        out_shape=jax.ShapeDtypeStruct((n_rows, dh), U32),
        compiler_params=pltpu.CompilerParams(dimension_semantics=("arbitrary",)),
        name="expert_dispatch",
    )(zstart, zlen, dest.reshape(n_steps, 1, td * TOP_K), tokp)


def _cast_rows(staging_ref, slot, resident_ref):
    def body(c, carry):
        rows = pl.ds(pl.multiple_of(c * CAST_ROWS, CAST_ROWS), CAST_ROWS)
        resident_ref[rows, :] = staging_ref[slot, rows, :].astype(BF16)
        return carry

    lax.fori_loop(0, resident_ref.shape[0] // CAST_ROWS, body, 0, unroll=2)


def _stream_expert_weights(be_ref, ord_ref, nxt_ref, meta_ref, n_sweeps, start, wait, cast):
    n = pl.program_id(0)
    j = pl.program_id(1)
    e = be_ref[j]
    used = j < meta_ref[0]
    changed = jnp.logical_and(used, jnp.logical_or(j == 0, e != be_ref[jnp.maximum(j - 1, 0)]))
    slot = (n * meta_ref[1] + ord_ref[j]) % 2

    @pl.when(jnp.logical_and(n == 0, j == 0))
    def _():
        start(e, n, slot)

    @pl.when(changed)
    def _():
        wait(e, n, slot)
        cast(slot)
        following = nxt_ref[j]

        @pl.when(following >= 0)
        def _():
            start(following, n, 1 - slot)

        @pl.when(jnp.logical_and(following < 0, n + 1 < n_sweeps))
        def _():
            start(be_ref[0], n + 1, 1 - slot)

    return used


def _expert_up_kernel(be_ref, ord_ref, nxt_ref, meta_ref, fill_ref, x_ref, wgu_ref, bgate_ref, bup_ref, h_ref,
                      wbuf_g, wbuf_u, wg_s, wu_s, sem, *, layer, n_sweeps):
    tn = h_ref.shape[1]

    def copies(e, n, slot):
        gate_cols = pl.ds(pl.multiple_of(n * tn, tn), tn)
        up_cols = pl.ds(pl.multiple_of((n_sweeps + n) * tn, tn), tn)
        return (pltpu.make_async_copy(wgu_ref.at[layer, e, :, gate_cols], wbuf_g.at[slot], sem.at[0, slot]),
                pltpu.make_async_copy(wgu_ref.at[layer, e, :, up_cols], wbuf_u.at[slot], sem.at[1, slot]))

    def start(e, n, slot):
        for cp in copies(e, n, slot):
            cp.start()

    def wait(e, n, slot):
        for cp in copies(e, n, slot):
            cp.wait()

    def cast(slot):
        _cast_rows(wbuf_g, slot, wg_s)
        _cast_rows(wbuf_u, slot, wu_s)

    used = _stream_expert_weights(be_ref, ord_ref, nxt_ref, meta_ref, n_sweeps, start, wait, cast)
    tm = h_ref.shape[0]

    def compute(n_rows):
        xp = x_ref[0:n_rows, :]
        half = xp.shape[1]
        xa = lax.bitcast_convert_type(xp & jnp.uint32(HI_MASK), F32).astype(BF16)
        xb = lax.bitcast_convert_type(xp << 16, F32).astype(BF16)
        cw = tn // UP_COL_CHUNKS
        for c in range(UP_COL_CHUNKS):
            cols = slice(c * cw, (c + 1) * cw)
            gate = (jnp.dot(xa, wg_s[0:half, cols], preferred_element_type=F32)
                    + jnp.dot(xb, wg_s[half:2 * half, cols], preferred_element_type=F32)) + bgate_ref[0, 0, :, cols]
            up = (jnp.dot(xa, wu_s[0:half, cols], preferred_element_type=F32)
                  + jnp.dot(xb, wu_s[half:2 * half, cols], preferred_element_type=F32)) + bup_ref[0, 0, :, cols]
            gate = jnp.minimum(gate, SWIGLU_LIMIT)
            up = jnp.clip(up, -SWIGLU_LIMIT, SWIGLU_LIMIT)
            act = (up + 1.0) * gate * jax.nn.sigmoid(SWIGLU_ALPHA * gate)
            h_ref[0:n_rows, cols] = act.astype(BF16)

    _rows_by_fill(used, fill_ref[pl.program_id(1)], tm, compute, h_ref)


def _rows_by_fill(used, filled_rows, tm, compute, out_ref):
    wide = jnp.logical_and(used, filled_rows > tm // 2)
    narrow = jnp.logical_and(used, filled_rows <= tm // 2)

    @pl.when(wide)
    def _():
        compute(tm)

    @pl.when(narrow)
    def _():
        compute(tm // 2)
        out_ref[tm // 2:tm, :] = jnp.zeros((tm - tm // 2, out_ref.shape[1]), out_ref.dtype)

    @pl.when(jnp.logical_not(used))
    def _():
        out_ref[...] = jnp.zeros_like(out_ref)


def _expert_up(xs, w_gu, b_gu, layer, plan):
    n_rows, dh = xs.shape
    d = 2 * dh
    _, n_exp, _, two_f = w_gu.shape
    f = two_f // 2
    tm = EXPERT_TILE
    tn = UP_COLS
    n_blocks = n_rows // tm
    nf = f // tn
    b4 = b_gu.reshape(b_gu.shape[0], n_exp, 1, two_f)
    return pl.pallas_call(
        functools.partial(_expert_up_kernel, layer=layer, n_sweeps=nf),
        grid_spec=pltpu.PrefetchScalarGridSpec(
            num_scalar_prefetch=5,
            grid=(nf, n_blocks),
            in_specs=[pl.BlockSpec((tm, dh), lambda n, j, be, od, nx, mt, fl: (j, 0)),
                      pl.BlockSpec(memory_space=pl.ANY),
                      pl.BlockSpec((1, 1, 1, tn), lambda n, j, be, od, nx, mt, fl: (layer, be[j], 0, n)),
                      pl.BlockSpec((1, 1, 1, tn), lambda n, j, be, od, nx, mt, fl: (layer, be[j], 0, nf + n))],
            out_specs=pl.BlockSpec((tm, tn), lambda n, j, be, od, nx, mt, fl: (j, n)),
            scratch_shapes=[pltpu.VMEM((2, d, tn), F32), pltpu.VMEM((2, d, tn), F32),
                            pltpu.VMEM((d, tn), BF16), pltpu.VMEM((d, tn), BF16),
                            pltpu.SemaphoreType.DMA((2, 2))],
        ),
        out_shape=jax.ShapeDtypeStruct((n_rows, f), BF16),
        compiler_params=pltpu.CompilerParams(
            dimension_semantics=("arbitrary", "arbitrary"), vmem_limit_bytes=EXPERT_VMEM_LIMIT),
        name="expert_up",
    )(*plan, xs, w_gu, b4, b4)


def _expert_down_kernel(be_ref, ord_ref, nxt_ref, meta_ref, fill_ref, h_ref, wd_ref, bd_ref, o_ref, wbuf, wd_s,
                        sem, *, layer):
    def copy(e, slot):
        return pltpu.make_async_copy(wd_ref.at[layer, e], wbuf.at[slot], sem.at[slot])

    def cast(slot):
        _cast_rows(wbuf, slot, wd_s)

    used = _stream_expert_weights(be_ref, ord_ref, nxt_ref, meta_ref, 1,
                                  lambda e, n, slot: copy(e, slot).start(),
                                  lambda e, n, slot: copy(e, slot).wait(), cast)

    def compute(n_rows):
        hb = h_ref[0:n_rows, :]
        half = o_ref.shape[1]
        cw = half // DOWN_COL_CHUNKS
        for c in range(DOWN_COL_CHUNKS):
            lo = slice(c * cw, (c + 1) * cw)
            hi = slice(half + c * cw, half + (c + 1) * cw)
            ya = jnp.dot(hb, wd_s[:, lo], preferred_element_type=F32) + bd_ref[0, 0, :, lo]
            yb = jnp.dot(hb, wd_s[:, hi], preferred_element_type=F32) + bd_ref[0, 0, :, hi]
            ba = lax.bitcast_convert_type(ya.astype(BF16).astype(F32), U32)
            bb = lax.bitcast_convert_type(yb.astype(BF16).astype(F32), U32)
            o_ref[0:n_rows, lo] = ba | (bb >> 16)

    _rows_by_fill(used, fill_ref[pl.program_id(1)], o_ref.shape[0], compute, o_ref)


def _expert_down(h, w_down, b_down, layer, plan):
    n_rows, f = h.shape
    _, n_exp, _, d = w_down.shape
    tm = EXPERT_TILE
    n_blocks = n_rows // tm
    return pl.pallas_call(
        functools.partial(_expert_down_kernel, layer=layer),
        grid_spec=pltpu.PrefetchScalarGridSpec(
            num_scalar_prefetch=5,
            grid=(1, n_blocks),
            in_specs=[pl.BlockSpec((tm, f), lambda n, j, be, od, nx, mt, fl: (j, 0)),
                      pl.BlockSpec(memory_space=pl.ANY),
                      pl.BlockSpec((1, 1, 1, d), lambda n, j, be, od, nx, mt, fl: (layer, be[j], 0, 0))],
            out_specs=pl.BlockSpec((tm, d // 2), lambda n, j, be, od, nx, mt, fl: (j, 0)),
            scratch_shapes=[pltpu.VMEM((2, f, d), F32), pltpu.VMEM((f, d), BF16), pltpu.SemaphoreType.DMA((2,))],
        ),
        out_shape=jax.ShapeDtypeStruct((n_rows, d // 2), U32),
        compiler_params=pltpu.CompilerParams(
            dimension_semantics=("arbitrary", "arbitrary"), vmem_limit_bytes=EXPERT_VMEM_LIMIT),
        name="expert_down",
    )(*plan, h, w_down, b_down.reshape(b_down.shape[0], n_exp, 1, d))


def _combine_kernel(destc_ref, destn_ref, outs_ref, w_ref, x_ref, mod_ref, gfin_ref, o_ref, buf, sem, *, final):
    t = pl.program_id(0)
    nt = pl.num_programs(0)
    tb = x_ref.shape[0]

    def issue(dest_ref, slot):
        def body(r, carry):
            for k in range(TOP_K):
                p = dest_ref[0, 0, r * TOP_K + k]
                pltpu.make_async_copy(outs_ref.at[pl.ds(p, 1)], buf.at[slot, k, pl.ds(r, 1)],
                                      sem.at[slot]).start(priority=k % DMA_QUEUES)
            return carry

        lax.fori_loop(0, tb, body, 0, unroll=4)

    @pl.when(t == 0)
    def _():
        issue(destc_ref, 0)

    @pl.when(t + 1 < nt)
    def _():
        issue(destn_ref, (t + 1) % 2)

    slot = t % 2

    def group(_, carry):
        for _ in range(WAIT_GROUP):
            pltpu.make_async_copy(outs_ref.at[pl.ds(0, 1)], buf.at[slot, 0, pl.ds(0, 1)], sem.at[slot]).wait()
        return carry

    lax.fori_loop(0, tb * TOP_K // WAIT_GROUP, group, 0)

    w = w_ref[...]
    half = buf.shape[3]
    ya = jnp.zeros((tb, half), F32)
    yb = jnp.zeros((tb, half), F32)
    for k in range(TOP_K):
        p = buf[slot, k]
        wk = w[:, k:k + 1]
        ya = ya + wk * lax.bitcast_convert_type(p & jnp.uint32(HI_MASK), F32)
        yb = yb + wk * lax.bitcast_convert_type(p << 16, F32)
    m = mod_ref[0, 0]
    xa = x_ref[:, 0:half] + m[5:6, 0:half] * ya
    xb = x_ref[:, half:2 * half] + m[5:6, half:2 * half] * yb
    if final:
        ms = (jnp.sum(xa * xa, axis=-1, keepdims=True) + jnp.sum(xb * xb, axis=-1, keepdims=True)) / (2 * half)
        scale = lax.rsqrt(ms + EPS)
        xa = xa * scale * gfin_ref[:, 0:half]
        xb = xb * scale * gfin_ref[:, half:2 * half]
    o_ref[:, 0:half] = xa
    o_ref[:, half:2 * half] = xb


def _combine(outs, dest, wt, x_flat, modt, g_final, tiles_per_batch, n_ctx_tiles, final):
    n_tok, d = x_flat.shape
    tb = COMBINE_TILE
    n_tiles = n_tok // tb
    cur = lambda t: (t, 0, 0)
    nxt = lambda t: (jnp.minimum(t + 1, n_tiles - 1), 0, 0)
    smem = lambda imap: pl.BlockSpec((1, 1, tb * TOP_K), imap, memory_space=pltpu.SMEM)
    dest3 = dest.reshape(n_tiles, 1, tb * TOP_K)
    return pl.pallas_call(
        functools.partial(_combine_kernel, final=final),
        grid=(n_tiles,),
        in_specs=[smem(cur), smem(nxt),
                  pl.BlockSpec(memory_space=pl.ANY),
                  pl.BlockSpec((tb, LANES), lambda t: (t, 0)),
                  pl.BlockSpec((tb, d), lambda t: (t, 0)),
                  pl.BlockSpec((1, 1, 6, d), lambda t: (
                      t // tiles_per_batch, jnp.where(t % tiles_per_batch < n_ctx_tiles, 0, 1), 0, 0)),
                  pl.BlockSpec((1, d), lambda t: (0, 0))],
        out_specs=pl.BlockSpec((tb, d), lambda t: (t, 0)),
        out_shape=jax.ShapeDtypeStruct((n_tok, d), F32),
        scratch_shapes=[pltpu.VMEM((2, TOP_K, tb, d // 2), U32), pltpu.SemaphoreType.DMA((2,))],
        compiler_params=pltpu.CompilerParams(
            dimension_semantics=("arbitrary",), vmem_limit_bytes=VMEM_LIMIT),
        name="expert_combine",
    )(dest3, dest3, outs, wt, x_flat, modt, g_final.reshape(1, d))


def _expert_layout(counts, n_assign):
    n_experts = counts.shape[0]
    tm = EXPERT_TILE
    padded = (counts + tm - 1) // tm * tm
    ends_p = jnp.cumsum(padded).astype(jnp.int32)
    starts_p = ends_p - padded
    n_blocks = (n_assign + n_experts * (tm - 1) + tm - 1) // tm
    block_start = jnp.arange(n_blocks, dtype=jnp.int32) * tm
    block_e = jnp.minimum(jnp.sum((ends_p[None, :] <= block_start[:, None]).astype(jnp.int32), axis=1),
                          n_experts - 1).astype(jnp.int32)
    n_used = ends_p[-1] // tm
    tile = jnp.arange(n_blocks, dtype=jnp.int32)
    used = tile < n_used
    first = jnp.logical_and(used, jnp.concatenate([jnp.ones((1,), bool), block_e[1:] != block_e[:-1]]))
    ordinal = (jnp.cumsum(first.astype(jnp.int32)) - 1).astype(jnp.int32)
    first_pos = jnp.where(first, tile, n_blocks)
    after = jnp.concatenate([first_pos[1:], jnp.full((1,), n_blocks, jnp.int32)])
    next_first = lax.cummin(after, axis=0, reverse=True)
    following = jnp.where(next_first < n_blocks, block_e[jnp.minimum(next_first, n_blocks - 1)], -1).astype(jnp.int32)
    meta = jnp.stack([n_used, jnp.sum(first.astype(jnp.int32))]).astype(jnp.int32)
    filled = jnp.clip((starts_p + counts)[block_e] - tile * tm, 0, tm).astype(jnp.int32)
    plan = (block_e, ordinal, following, meta, filled)
    return starts_p, starts_p + counts, padded - counts, plan, n_blocks * tm


def _moe(tokp, route, counts, w_gu, b_gu, w_down, b_down, layer):
    n_tok = tokp.shape[0]
    n_experts = counts.shape[0]
    idx4 = route[:, :TOP_K]
    rank4 = route[:, TOP_K:2 * TOP_K]
    starts_p, zstart, zlen, plan, n_rows = _expert_layout(counts, n_tok * TOP_K)
    onehot = idx4[:, :, None] == jnp.arange(n_experts, dtype=jnp.int32)[None, None, :]
    dest = (jnp.sum(jnp.where(onehot, starts_p[None, None, :], 0), axis=-1) + rank4).astype(jnp.int32)
    xs = _dispatch(tokp, dest, zstart, zlen, n_rows)
    h = _expert_up(xs, w_gu, b_gu, layer, plan)
    outs = _expert_down(h, w_down, b_down, layer, plan)
    return outs, dest


def kernel(x, c, ctx, c_ctx, w_mod, b_mod, w_in, conv_w, conv_b, w_r, b_r, w_i, b_i, lam, g_v, w_s, b_s,
           g_lru, g_mlp, w_out, w_router, b_router, w_gu, b_gu, w_down, b_down, g_final):
    bsz, n_lat, d = x.shape
    n_ctx = ctx.shape[1]
    depth = w_mod.shape[0]
    n_experts = w_router.shape[2]
    r = conv_w.shape[2]
    tt = TIME_TILE
    assert n_ctx % tt == 0 and n_lat % tt == 0 and tt % CHUNK == 0
    assert tt % DISPATCH_TILE == 0 and tt % COMBINE_TILE == 0
    assert n_experts <= LANES and r % LRU_HEADS == 0
    n_ctx_tiles = n_ctx // tt

    rows = (bsz + 1 + SUBLANES - 1) // SUBLANES * SUBLANES
    c_all = jnp.zeros((rows, d), F32).at[:bsz].set(c).at[bsz].set(c_ctx)
    mod = _modulation(c_all, w_mod, b_mod)

    hcat = jnp.concatenate([ctx, x], axis=1)
    out = None
    for l in range(depth):
        last = l == depth - 1
        mod_x = mod[l, :bsz].reshape(bsz, 1, 6, d)
        mod_c = jnp.broadcast_to(mod[l, bsz].reshape(1, 1, 6, d), (bsz, 1, 6, d))
        modt = jnp.concatenate([mod_c, mod_x], axis=1)

        z = _inproj(hcat, modt, w_in[l].astype(BF16), n_ctx_tiles)
        hd = r // LRU_HEADS
        wg = jnp.concatenate([w_r[l], w_i[l]], axis=-1).astype(BF16)
        bg = jnp.concatenate([b_r[l], b_i[l]], axis=-1).reshape(2, LRU_HEADS, 1, 2 * hd)
        hf, hb = _scan(z, conv_w[l], conv_b[l].reshape(1, r), wg, bg, lam[l], n_ctx_tiles)

        wr_hi = w_router[l].astype(BF16)
        wr_lo = (w_router[l] - wr_hi.astype(F32)).astype(BF16)
        wr_split = jnp.pad(jnp.stack([wr_hi, wr_lo]), ((0, 0), (0, 0), (0, LANES - n_experts)))
        br_pad = jnp.pad(b_router[l], (0, LANES - n_experts)).reshape(1, LANES)
        off = n_ctx_tiles if last else 0
        xo, tokp, route, wt, cnt = _finish(
            z, hf, hb, hcat, modt, g_lru[l].reshape(1, r), g_mlp[l].reshape(1, r), g_v[l].reshape(1, r),
            w_s[l].astype(BF16), b_s[l].reshape(MLP_HEADS, CHUNK, 1), w_out[l].astype(BF16),
            wr_split, br_pad, n_experts, off, n_ctx_tiles)

        rows_out = xo.shape[1]
        n_tok = bsz * rows_out
        outs, dest = _moe(
            tokp.reshape(n_tok, d // 2), route.reshape(n_tok, LANES), cnt[0, :n_experts],
            w_gu, b_gu, w_down, b_down, l)
        new = _combine(outs, dest, wt.reshape(n_tok, LANES), xo.reshape(n_tok, d), modt,
                       g_final, rows_out // COMBINE_TILE, 0 if last else n_ctx // COMBINE_TILE, last)
        new = new.reshape(bsz, rows_out, d)
        if last:
            out = new
        else:
            hcat = new
    return out
```

```python
import functools
import math

import jax
import jax.numpy as jnp
from jax import lax
from jax.experimental import pallas as pl
from jax.experimental.pallas import tpu as pltpu

LRU_HEADS = 8
MLP_HEADS = 8
CONV_W = 4
CONV_LEFT = 2
RG_C = 8.0
CHUNK = 128
TOP_K = 4
SWIGLU_LIMIT = 7.0
SWIGLU_ALPHA = 1.702
EPS = 1e-6

SUBLANES = 8
LANES = 128
TIME_TILE = 256
EXPERT_TILE = 512
DISPATCH_TILE = 256
COMBINE_TILE = 128
WAIT_GROUP = 64
DMA_QUEUES = 2
UP_COLS = 1024
UP_COL_CHUNKS = 2
DOWN_COL_CHUNKS = 2
CAST_ROWS = 32
VMEM_LIMIT = 56 * 1024 * 1024
EXPERT_VMEM_LIMIT = 60 * 1024 * 1024

F32 = jnp.float32
BF16 = jnp.bfloat16
U32 = jnp.uint32
HI_MASK = 0xFFFF0000


def _rms(x):
    return x * lax.rsqrt(jnp.mean(x * x, axis=-1, keepdims=True) + EPS)


def _gelu(x):
    c = math.sqrt(2.0 / math.pi)
    return 0.5 * x * (1.0 + jnp.tanh(c * (x + 0.044715 * (x * x * x))))


def _softplus(x):
    return jnp.maximum(x, 0.0) + jnp.log1p(jnp.exp(-jnp.abs(x)))


def _mod_kernel(c_ref, w_ref, b_ref, o_ref):
    c = c_ref[...]
    s = c * jax.nn.sigmoid(c)
    o_ref[0] = jnp.dot(s, w_ref[0], preferred_element_type=F32) + b_ref[0]


def _modulation(c_all, w_mod, b_mod):
    depth, d, n6 = w_mod.shape
    rows = c_all.shape[0]
    tn = n6 // 8
    return pl.pallas_call(
        _mod_kernel,
        grid=(depth, n6 // tn),
        in_specs=[
            pl.BlockSpec((rows, d), lambda l, n: (0, 0)),
            pl.BlockSpec((1, d, tn), lambda l, n: (l, 0, n)),
            pl.BlockSpec((1, 1, tn), lambda l, n: (l, 0, n)),
        ],
        out_specs=pl.BlockSpec((1, rows, tn), lambda l, n: (l, 0, n)),
        out_shape=jax.ShapeDtypeStruct((depth, rows, n6), F32),
        compiler_params=pltpu.CompilerParams(
            dimension_semantics=("arbitrary", "arbitrary"), vmem_limit_bytes=VMEM_LIMIT),
        name="modulation",
    )(c_all, w_mod, b_mod.reshape(depth, 1, n6))


def _inproj_kernel(x_ref, mod_ref, w_ref, z_ref, *, n_chunks):
    x = x_ref[0]
    m = mod_ref[0, 0]
    nx = _rms(x) * (1.0 + m[1:2]) + m[0:1]
    nb = nx.astype(BF16)
    cw = w_ref.shape[1] // n_chunks
    for c in range(n_chunks):
        z_ref[0, :, c * cw:(c + 1) * cw] = jnp.dot(
            nb, w_ref[:, c * cw:(c + 1) * cw], preferred_element_type=F32)


def _inproj(x, modt, w_in_bf16, n_ctx_tiles):
    bsz, s, d = x.shape
    n_cols = w_in_bf16.shape[1]
    tt = TIME_TILE
    return pl.pallas_call(
        functools.partial(_inproj_kernel, n_chunks=4),
        grid=(bsz, s // tt),
        in_specs=[
            pl.BlockSpec((1, tt, d), lambda b, i: (b, i, 0)),
            pl.BlockSpec((1, 1, 6, d), lambda b, i: (b, jnp.where(i < n_ctx_tiles, 0, 1), 0, 0)),
            pl.BlockSpec((d, n_cols), lambda b, i: (0, 0), pipeline_mode=pl.Buffered(1)),
        ],
        out_specs=pl.BlockSpec((1, tt, n_cols), lambda b, i: (b, i, 0)),
        out_shape=jax.ShapeDtypeStruct((bsz, s, n_cols), F32),
        compiler_params=pltpu.CompilerParams(
            dimension_semantics=("parallel", "arbitrary"), vmem_limit_bytes=VMEM_LIMIT),
        name="inproj",
    )(x, modt, w_in_bf16)


def _conv_tile(main, prev8, next8, cw, cb):
    tt = main.shape[0]
    row8 = lax.broadcasted_iota(jnp.int32, (SUBLANES, main.shape[1]), 0)
    w0, w1, w2, w3 = cw[0:1], cw[1:2], cw[2:3], cw[3:4]
    r1 = pltpu.roll(main, 1, 0)
    r2 = pltpu.roll(main, 2, 0)
    rm1 = pltpu.roll(main, tt - 1, 0)
    body = cb + w0 * r2 + w1 * r1 + w2 * main + w3 * rm1
    head = main[0:SUBLANES]
    h1 = jnp.where(row8 < 1, pltpu.roll(prev8, 1, 0), pltpu.roll(head, 1, 0))
    h2 = jnp.where(row8 < 2, pltpu.roll(prev8, 2, 0), pltpu.roll(head, 2, 0))
    head_out = cb + w0 * h2 + w1 * h1 + w2 * head + w3 * rm1[0:SUBLANES]
    tail_m1 = jnp.where(row8 == SUBLANES - 1, pltpu.roll(next8, SUBLANES - 1, 0), rm1[tt - SUBLANES:tt])
    tail_out = (cb + w0 * r2[tt - SUBLANES:tt] + w1 * r1[tt - SUBLANES:tt]
                + w2 * main[tt - SUBLANES:tt] + w3 * tail_m1)
    return body, head_out, tail_out


def _gates_tile(rc_ref, a_ref, u_ref, wg_ref, bg_ref, sp, direction):
    hd = rc_ref.shape[1] // LRU_HEADS
    for h in range(LRU_HEADS):
        cols = slice(h * hd, (h + 1) * hd)
        xh = rc_ref[:, cols]
        pre = jnp.dot(xh.astype(BF16), wg_ref[direction, h], preferred_element_type=F32) + bg_ref[direction, h]
        r = jax.nn.sigmoid(pre[:, :hd])
        gi = jax.nn.sigmoid(pre[:, hd:])
        a = jnp.exp((-RG_C) * r * sp[:, cols])
        a_ref[:, cols] = a
        u_ref[:, cols] = jnp.sqrt(1.0 - a * a) * gi * xh


def _scan_tile(a_ref, u_ref, out_ref, carry_ref, reverse):
    tt, width = a_ref.shape
    groups = tt // SUBLANES
    row8 = lax.broadcasted_iota(jnp.int32, (SUBLANES, width), 0)

    def body(g, carry):
        gi = (groups - 1 - g) if reverse else g
        sl = pl.ds(pl.multiple_of(gi * SUBLANES, SUBLANES), SUBLANES)
        a = a_ref[sl, :]
        u = u_ref[sl, :]
        for k in (1, 2, 4):
            if reverse:
                shift = SUBLANES - k
                valid = row8 < SUBLANES - k
            else:
                shift = k
                valid = row8 >= k
            a_s = pltpu.roll(a, shift, 0)
            u_s = pltpu.roll(u, shift, 0)
            u = jnp.where(valid, a * u_s + u, u)
            a = jnp.where(valid, a * a_s, a)
        h = a * carry + u
        out_ref[0, sl, :] = h
        return h[0:1] if reverse else h[SUBLANES - 1:SUBLANES]

    carry_ref[...] = lax.fori_loop(0, groups, body, carry_ref[...])


def _scan_kernel(zf_ref, zfp_ref, zfn_ref, zb_ref, zbp_ref, zbn_ref, cw_ref, cb_ref, wg_ref, bg_ref,
                 lam_ref, hf_ref, hb_ref, rc_ref, a_ref, u_ref, cf_ref, cbk_ref, *, n_ctx_tiles, n_tiles):
    i = pl.program_id(1)
    tt = rc_ref.shape[0]

    @pl.when(i == 0)
    def _():
        cf_ref[...] = jnp.zeros_like(cf_ref)
        cbk_ref[...] = jnp.zeros_like(cbk_ref)

    jb = jnp.where(i < n_ctx_tiles, n_ctx_tiles - 1 - i, n_tiles - 1 - (i - n_ctx_tiles))
    cw = cw_ref[...]
    cb = cb_ref[...]
    sp = _softplus(-lam_ref[...])

    def run(z_ref, zp_ref, zn_ref, j, direction, out_ref, carry_ref):
        first = jnp.logical_or(j == 0, j == n_ctx_tiles)
        last = jnp.logical_or(j == n_ctx_tiles - 1, j == n_tiles - 1)
        prev8 = zp_ref[0] * jnp.where(first, 0.0, 1.0)
        next8 = zn_ref[0] * jnp.where(last, 0.0, 1.0)
        body, head_out, tail_out = _conv_tile(z_ref[0], prev8, next8, cw, cb)
        rc_ref[...] = body
        rc_ref[0:SUBLANES] = head_out
        rc_ref[tt - SUBLANES:tt] = tail_out
        _gates_tile(rc_ref, a_ref, u_ref, wg_ref, bg_ref, sp[direction:direction + 1], direction)
        _scan_tile(a_ref, u_ref, out_ref, carry_ref, reverse=(direction == 1))

    run(zf_ref, zfp_ref, zfn_ref, i, 0, hf_ref, cf_ref)
    run(zb_ref, zbp_ref, zbn_ref, jb, 1, hb_ref, cbk_ref)


def _scan(z, conv_w, conv_b, wg, bg, lam, n_ctx_tiles):
    bsz, s, _ = z.shape
    r = conv_w.shape[1]
    tt = TIME_TILE
    n_tiles = s // tt
    per = tt // SUBLANES
    last8 = s // SUBLANES - 1

    def bwd_tile(i):
        return jnp.where(i < n_ctx_tiles, n_ctx_tiles - 1 - i, n_tiles - 1 - (i - n_ctx_tiles))

    main_f = pl.BlockSpec((1, tt, r), lambda b, i: (b, i, 0))
    prev_f = pl.BlockSpec((1, SUBLANES, r), lambda b, i: (b, jnp.maximum(i * per - 1, 0), 0))
    next_f = pl.BlockSpec((1, SUBLANES, r), lambda b, i: (b, jnp.minimum((i + 1) * per, last8), 0))
    main_b = pl.BlockSpec((1, tt, r), lambda b, i: (b, bwd_tile(i), 0))
    prev_b = pl.BlockSpec((1, SUBLANES, r), lambda b, i: (b, jnp.maximum(bwd_tile(i) * per - 1, 0), 0))
    next_b = pl.BlockSpec((1, SUBLANES, r), lambda b, i: (b, jnp.minimum((bwd_tile(i) + 1) * per, last8), 0))
    full = lambda shape: pl.BlockSpec(shape, lambda b, i: (0,) * len(shape))
    return pl.pallas_call(
        functools.partial(_scan_kernel, n_ctx_tiles=n_ctx_tiles, n_tiles=n_tiles),
        grid=(bsz, n_tiles),
        in_specs=[main_f, prev_f, next_f, main_b, prev_b, next_b,
                  full(conv_w.shape), full(conv_b.shape), full(wg.shape), full(bg.shape), full(lam.shape)],
        out_specs=[pl.BlockSpec((1, tt, r), lambda b, i: (b, i, 0)),
                   pl.BlockSpec((1, tt, r), lambda b, i: (b, bwd_tile(i), 0))],
        out_shape=[jax.ShapeDtypeStruct((bsz, s, r), F32), jax.ShapeDtypeStruct((bsz, s, r), F32)],
        scratch_shapes=[pltpu.VMEM((tt, r), F32), pltpu.VMEM((tt, r), F32), pltpu.VMEM((tt, r), F32),
                        pltpu.VMEM((1, r), F32), pltpu.VMEM((1, r), F32)],
        compiler_params=pltpu.CompilerParams(
            dimension_semantics=("parallel", "arbitrary"), vmem_limit_bytes=VMEM_LIMIT),
        name="lru_scan",
    )(z, z, z, z, z, z, conv_w, conv_b, wg, bg, lam)


def _finish_kernel(zg_ref, zu_ref, zv_ref, hf_ref, hb_ref, x_ref, mod_ref, glru_ref, gmlp_ref, gv_ref,
                   ws_ref, bs_ref, wout_ref, wr_ref, br_ref,
                   xo_ref, tokp_ref, route_ref, wt_ref, cnt_ref, ym_ref, run_ref, *, n_experts):
    tt, r = ym_ref.shape
    hd = r // MLP_HEADS

    @pl.when(jnp.logical_and(pl.program_id(0) == 0, pl.program_id(1) == 0))
    def _():
        run_ref[...] = jnp.zeros_like(run_ref)

    y_lru = (hf_ref[0] + hb_ref[0]) * _gelu(zg_ref[0])
    yl = _rms(y_lru) * glru_ref[...]
    u = _gelu(zu_ref[0])
    vb = (_rms(_gelu(zv_ref[0])) * gv_ref[...]).astype(BF16)
    for ch in range(tt // CHUNK):
        rows = slice(ch * CHUNK, (ch + 1) * CHUNK)
        for h in range(MLP_HEADS):
            cols = slice(h * hd, (h + 1) * hd)
            mixed = jnp.dot(ws_ref[h], vb[rows, cols], preferred_element_type=F32) + bs_ref[h]
            ym_ref[rows, cols] = u[rows, cols] * mixed
    ym = _rms(ym_ref[...]) * gmlp_ref[...]
    y = (jnp.dot(yl.astype(BF16), wout_ref[0:r], preferred_element_type=F32)
         + jnp.dot(ym.astype(BF16), wout_ref[r:2 * r], preferred_element_type=F32))
    m = mod_ref[0, 0]
    xnew = x_ref[0] + m[2:3] * y
    xo_ref[0] = xnew
    tok = _rms(xnew) * (1.0 + m[4:5]) + m[3:4]

    half = tok.shape[1] // 2
    t_hi = tok.astype(BF16)
    t_hi32 = t_hi.astype(F32)
    bits = lax.bitcast_convert_type(t_hi32, U32)
    tokp_ref[0] = bits[:, :half] | (bits[:, half:] >> 16)

    t_lo = (tok - t_hi32).astype(BF16)
    logits = (jnp.dot(t_hi, wr_ref[0], preferred_element_type=F32)
              + jnp.dot(t_lo, wr_ref[0], preferred_element_type=F32)
              + jnp.dot(t_hi, wr_ref[1], preferred_element_type=F32)) + br_ref[...]
    lane = lax.broadcasted_iota(jnp.int32, logits.shape, 1)
    neg = jnp.float32(-jnp.inf)
    work = jnp.where(lane < n_experts, logits, neg)
    vals, idxs = [], []
    for _ in range(TOP_K):
        mx = jnp.max(work, axis=-1, keepdims=True)
        ix = jnp.min(jnp.where(work == mx, lane, LANES), axis=-1, keepdims=True)
        vals.append(mx)
        idxs.append(ix)
        work = jnp.where(lane == ix, neg, work)
    exps = [jnp.exp(v - vals[0]) for v in vals]
    denom = exps[0] + exps[1] + exps[2] + exps[3]

    hot = [lane == ix for ix in idxs]
    multi = jnp.zeros(logits.shape, F32)
    for k in range(TOP_K):
        multi = multi + hot[k].astype(F32)
    before = (lax.broadcasted_iota(jnp.int32, (tt, tt), 1) < lax.broadcasted_iota(jnp.int32, (tt, tt), 0))
    excl = jnp.dot(before.astype(BF16), multi.astype(BF16), preferred_element_type=F32) + run_ref[...]
    run_ref[...] = run_ref[...] + jnp.sum(multi, axis=0, keepdims=True)
    cnt_ref[...] = jnp.broadcast_to(run_ref[...], cnt_ref.shape).astype(jnp.int32)

    route = jnp.zeros(logits.shape, jnp.int32)
    wt_out = jnp.zeros(logits.shape, F32)
    for k in range(TOP_K):
        rank = jnp.sum(jnp.where(hot[k], excl, 0.0), axis=-1, keepdims=True).astype(jnp.int32)
        route = jnp.where(lane == k, idxs[k], route)
        route = jnp.where(lane == TOP_K + k, rank, route)
        wt_out = jnp.where(lane == k, exps[k] / denom, wt_out)
    route_ref[0] = route
    wt_ref[0] = wt_out


def _finish(z, hf, hb, x, modt, g_lru, g_mlp, g_v, ws, bs, w_out_bf16, wr_split, br_pad, n_experts,
            tile_offset, n_ctx_tiles):
    bsz, s, d = x.shape
    r = hf.shape[2]
    tt = TIME_TILE
    n_out_tiles = s // tt - tile_offset
    rows_out = n_out_tiles * tt
    off = tile_offset

    def zcol(c):
        return pl.BlockSpec((1, tt, r), lambda b, i: (b, i + off, c))

    full = lambda shape: pl.BlockSpec(shape, lambda b, i: (0,) * len(shape))
    tile_in = lambda w: pl.BlockSpec((1, tt, w), lambda b, i: (b, i + off, 0))
    tile_out = lambda w: pl.BlockSpec((1, tt, w), lambda b, i: (b, i, 0))
    return pl.pallas_call(
        functools.partial(_finish_kernel, n_experts=n_experts),
        grid=(bsz, n_out_tiles),
        in_specs=[zcol(1), zcol(2), zcol(3), tile_in(r), tile_in(r), tile_in(d),
                  pl.BlockSpec((1, 1, 6, d), lambda b, i: (b, jnp.where(i + off < n_ctx_tiles, 0, 1), 0, 0)),
                  full(g_lru.shape), full(g_mlp.shape), full(g_v.shape), full(ws.shape), full(bs.shape),
                  pl.BlockSpec(w_out_bf16.shape, lambda b, i: (0, 0), pipeline_mode=pl.Buffered(1)),
                  full(wr_split.shape), full(br_pad.shape)],
        out_specs=[tile_out(d), tile_out(d // 2), tile_out(LANES), tile_out(LANES),
                   pl.BlockSpec((SUBLANES, LANES), lambda b, i: (0, 0))],
        out_shape=[jax.ShapeDtypeStruct((bsz, rows_out, d), F32),
                   jax.ShapeDtypeStruct((bsz, rows_out, d // 2), U32),
                   jax.ShapeDtypeStruct((bsz, rows_out, LANES), jnp.int32),
                   jax.ShapeDtypeStruct((bsz, rows_out, LANES), F32),
                   jax.ShapeDtypeStruct((SUBLANES, LANES), jnp.int32)],
        scratch_shapes=[pltpu.VMEM((tt, r), F32), pltpu.VMEM((1, LANES), F32)],
        compiler_params=pltpu.CompilerParams(
            dimension_semantics=("arbitrary", "arbitrary"), vmem_limit_bytes=VMEM_LIMIT),
        name="mixer_finish",
    )(z, z, z, hf, hb, x, modt, g_lru, g_mlp, g_v, ws, bs, w_out_bf16, wr_split, br_pad)


def _row_copy(src_ref, src_row, dst_ref, dst_row, sem):
    return pltpu.make_async_copy(src_ref.at[pl.ds(src_row, 1)], dst_ref.at[pl.ds(dst_row, 1)], sem)


def _wait_rows(src_ref, dst_ref, sem, n_rows):
    def group(_, carry):
        for _ in range(WAIT_GROUP):
            _row_copy(src_ref, 0, dst_ref, 0, sem).wait()
        return carry

    lax.fori_loop(0, n_rows // WAIT_GROUP, group, 0)


def _zero_pieces(tile):
    return [1 << b for b in range(tile.bit_length() - 2, 2, -1)]


def _dispatch_kernel(zstart_ref, zlen_ref, dest_ref, tok_ref, xs_ref, zbuf, sem, zsem, *, n_experts):
    s = pl.program_id(0)
    td = dest_ref.shape[2] // TOP_K
    pieces = _zero_pieces(EXPERT_TILE)

    @pl.when(s == 0)
    def _():
        zbuf[...] = jnp.zeros_like(zbuf)

        def fill(e, carry):
            gap = zlen_ref[e]
            start = zstart_ref[e]
            lead = jnp.minimum((SUBLANES - (start & (SUBLANES - 1))) & (SUBLANES - 1), gap)
            for i in range(SUBLANES - 1):
                @pl.when(i < lead)
                def _():
                    _row_copy(zbuf, 0, xs_ref, start + i, zsem).start()
            rest = gap - lead
            off = start + lead
            for p in pieces:
                @pl.when((rest & p) != 0)
                def _():
                    pltpu.make_async_copy(zbuf.at[pl.ds(0, p)],
                                          xs_ref.at[pl.ds(pl.multiple_of(off, SUBLANES), p)], zsem).start()

                off = off + (rest & p)
            for i in range(SUBLANES - 1):
                @pl.when(i < lead)
                def _():
                    _row_copy(zbuf, 0, xs_ref, 0, zsem).wait()
            for p in pieces:
                @pl.when((rest & p) != 0)
                def _():
                    pltpu.make_async_copy(zbuf.at[pl.ds(0, p)], xs_ref.at[pl.ds(0, p)], zsem).wait()
            return carry

        lax.fori_loop(0, n_experts, fill, 0)

        zrows = zbuf.shape[0]
        first_free = (zstart_ref[n_experts - 1] + zlen_ref[n_experts - 1]) // zrows
        n_chunks = xs_ref.shape[0] // zrows

        def fill_tail(c, carry):
            row = pl.multiple_of(c * zrows, zrows)
            cp = pltpu.make_async_copy(zbuf, xs_ref.at[pl.ds(row, zrows)], zsem)
            cp.start()
            cp.wait()
            return carry

        lax.fori_loop(first_free, n_chunks, fill_tail, 0)

    def issue(t, carry):
        for k in range(TOP_K):
            _row_copy(tok_ref, t, xs_ref, dest_ref[0, 0, t * TOP_K + k], sem).start(priority=k % DMA_QUEUES)
        return carry

    lax.fori_loop(0, td, issue, 0, unroll=4)
    _wait_rows(tok_ref, xs_ref, sem, td * TOP_K)


def _dispatch(tokp, dest, zstart, zlen, n_rows):
    n_tok, dh = tokp.shape
    td = DISPATCH_TILE
    n_steps = n_tok // td
    n_experts = zstart.shape[0]
    zrows = _zero_pieces(EXPERT_TILE)[0]
    return pl.pallas_call(
        functools.partial(_dispatch_kernel, n_experts=n_experts),
        grid_spec=pltpu.PrefetchScalarGridSpec(
            num_scalar_prefetch=2,
            grid=(n_steps,),
            in_specs=[pl.BlockSpec((1, 1, td * TOP_K), lambda s, a, b: (s, 0, 0), memory_space=pltpu.SMEM),
                      pl.BlockSpec((td, dh), lambda s, a, b: (s, 0))],
            out_specs=pl.BlockSpec(memory_space=pl.ANY),
            scratch_shapes=[pltpu.VMEM((zrows, dh), U32), pltpu.SemaphoreType.DMA(()),
                            pltpu.SemaphoreType.DMA(())],
        ),
        out_shape=jax.ShapeDtypeStruct((n_rows, dh), U32),
        compiler_params=pltpu.CompilerParams(dimension_semantics=("arbitrary",)),
        name="expert_dispatch",
    )(zstart, zlen, dest.reshape(n_steps, 1, td * TOP_K), tokp)


def _cast_rows(staging_ref, slot, resident_ref):
    def body(c, carry):
        rows = pl.ds(pl.multiple_of(c * CAST_ROWS, CAST_ROWS), CAST_ROWS)
        resident_ref[rows, :] = staging_ref[slot, rows, :].astype(BF16)
        return carry

    lax.fori_loop(0, resident_ref.shape[0] // CAST_ROWS, body, 0, unroll=2)


def _stream_expert_weights(be_ref, ord_ref, nxt_ref, meta_ref, n_sweeps, start, wait, cast):
    n = pl.program_id(0)
    j = pl.program_id(1)
    e = be_ref[j]
    used = j < meta_ref[0]
    changed = jnp.logical_and(used, jnp.logical_or(j == 0, e != be_ref[jnp.maximum(j - 1, 0)]))
    slot = (n * meta_ref[1] + ord_ref[j]) % 2

    @pl.when(jnp.logical_and(n == 0, j == 0))
    def _():
        start(e, n, slot)

    @pl.when(changed)
    def _():
        wait(e, n, slot)
        cast(slot)
        following = nxt_ref[j]

        @pl.when(following >= 0)
        def _():
            start(following, n, 1 - slot)

        @pl.when(jnp.logical_and(following < 0, n + 1 < n_sweeps))
        def _():
            start(be_ref[0], n + 1, 1 - slot)

    return used


def _expert_up_kernel(be_ref, ord_ref, nxt_ref, meta_ref, fill_ref, x_ref, wgu_ref, bgate_ref, bup_ref, h_ref,
                      wbuf_g, wbuf_u, wg_s, wu_s, sem, *, layer, n_sweeps):
    tn = h_ref.shape[1]

    def copies(e, n, slot):
        gate_cols = pl.ds(pl.multiple_of(n * tn, tn), tn)
        up_cols = pl.ds(pl.multiple_of((n_sweeps + n) * tn, tn), tn)
        return (pltpu.make_async_copy(wgu_ref.at[layer, e, :, gate_cols], wbuf_g.at[slot], sem.at[0, slot]),
                pltpu.make_async_copy(wgu_ref.at[layer, e, :, up_cols], wbuf_u.at[slot], sem.at[1, slot]))

    def start(e, n, slot):
        for cp in copies(e, n, slot):
            cp.start()

    def wait(e, n, slot):
        for cp in copies(e, n, slot):
            cp.wait()

    def cast(slot):
        _cast_rows(wbuf_g, slot, wg_s)
        _cast_rows(wbuf_u, slot, wu_s)

    used = _stream_expert_weights(be_ref, ord_ref, nxt_ref, meta_ref, n_sweeps, start, wait, cast)
    tm = h_ref.shape[0]

    def compute(n_rows):
        xp = x_ref[0:n_rows, :]
        half = xp.shape[1]
        xa = lax.bitcast_convert_type(xp & jnp.uint32(HI_MASK), F32).astype(BF16)
        xb = lax.bitcast_convert_type(xp << 16, F32).astype(BF16)
        cw = tn // UP_COL_CHUNKS
        for c in range(UP_COL_CHUNKS):
            cols = slice(c * cw, (c + 1) * cw)
            gate = (jnp.dot(xa, wg_s[0:half, cols], preferred_element_type=F32)
                    + jnp.dot(xb, wg_s[half:2 * half, cols], preferred_element_type=F32)) + bgate_ref[0, 0, :, cols]
            up = (jnp.dot(xa, wu_s[0:half, cols], preferred_element_type=F32)
                  + jnp.dot(xb, wu_s[half:2 * half, cols], preferred_element_type=F32)) + bup_ref[0, 0, :, cols]
            gate = jnp.minimum(gate, SWIGLU_LIMIT)
            up = jnp.clip(up, -SWIGLU_LIMIT, SWIGLU_LIMIT)
            act = (up + 1.0) * gate * jax.nn.sigmoid(SWIGLU_ALPHA * gate)
            h_ref[0:n_rows, cols] = act.astype(BF16)

    _rows_by_fill(used, fill_ref[pl.program_id(1)], tm, compute, h_ref)


def _rows_by_fill(used, filled_rows, tm, compute, out_ref):
    wide = jnp.logical_and(used, filled_rows > tm // 2)
    narrow = jnp.logical_and(used, filled_rows <= tm // 2)

    @pl.when(wide)
    def _():
        compute(tm)

    @pl.when(narrow)
    def _():
        compute(tm // 2)
        out_ref[tm // 2:tm, :] = jnp.zeros((tm - tm // 2, out_ref.shape[1]), out_ref.dtype)

    @pl.when(jnp.logical_not(used))
    def _():
        out_ref[...] = jnp.zeros_like(out_ref)


def _expert_up(xs, w_gu, b_gu, layer, plan):
    n_rows, dh = xs.shape
    d = 2 * dh
    _, n_exp, _, two_f = w_gu.shape
    f = two_f // 2
    tm = EXPERT_TILE
    tn = UP_COLS
    n_blocks = n_rows // tm
    nf = f // tn
    b4 = b_gu.reshape(b_gu.shape[0], n_exp, 1, two_f)
    return pl.pallas_call(
        functools.partial(_expert_up_kernel, layer=layer, n_sweeps=nf),
        grid_spec=pltpu.PrefetchScalarGridSpec(
            num_scalar_prefetch=5,
            grid=(nf, n_blocks),
            in_specs=[pl.BlockSpec((tm, dh), lambda n, j, be, od, nx, mt, fl: (j, 0)),
                      pl.BlockSpec(memory_space=pl.ANY),
                      pl.BlockSpec((1, 1, 1, tn), lambda n, j, be, od, nx, mt, fl: (layer, be[j], 0, n)),
                      pl.BlockSpec((1, 1, 1, tn), lambda n, j, be, od, nx, mt, fl: (layer, be[j], 0, nf + n))],
            out_specs=pl.BlockSpec((tm, tn), lambda n, j, be, od, nx, mt, fl: (j, n)),
            scratch_shapes=[pltpu.VMEM((2, d, tn), F32), pltpu.VMEM((2, d, tn), F32),
                            pltpu.VMEM((d, tn), BF16), pltpu.VMEM((d, tn), BF16),
                            pltpu.SemaphoreType.DMA((2, 2))],
        ),
        out_shape=jax.ShapeDtypeStruct((n_rows, f), BF16),
        compiler_params=pltpu.CompilerParams(
            dimension_semantics=("arbitrary", "arbitrary"), vmem_limit_bytes=EXPERT_VMEM_LIMIT),
        name="expert_up",
    )(*plan, xs, w_gu, b4, b4)


def _expert_down_kernel(be_ref, ord_ref, nxt_ref, meta_ref, fill_ref, h_ref, wd_ref, bd_ref, o_ref, wbuf, wd_s,
                        sem, *, layer):
    def copy(e, slot):
        return pltpu.make_async_copy(wd_ref.at[layer, e], wbuf.at[slot], sem.at[slot])

    def cast(slot):
        _cast_rows(wbuf, slot, wd_s)

    used = _stream_expert_weights(be_ref, ord_ref, nxt_ref, meta_ref, 1,
                                  lambda e, n, slot: copy(e, slot).start(),
                                  lambda e, n, slot: copy(e, slot).wait(), cast)

    def compute(n_rows):
        hb = h_ref[0:n_rows, :]
        half = o_ref.shape[1]
        cw = half // DOWN_COL_CHUNKS
        for c in range(DOWN_COL_CHUNKS):
            lo = slice(c * cw, (c + 1) * cw)
            hi = slice(half + c * cw, half + (c + 1) * cw)
            ya = jnp.dot(hb, wd_s[:, lo], preferred_element_type=F32) + bd_ref[0, 0, :, lo]
            yb = jnp.dot(hb, wd_s[:, hi], preferred_element_type=F32) + bd_ref[0, 0, :, hi]
            ba = lax.bitcast_convert_type(ya.astype(BF16).astype(F32), U32)
            bb = lax.bitcast_convert_type(yb.astype(BF16).astype(F32), U32)
            o_ref[0:n_rows, lo] = ba | (bb >> 16)

    _rows_by_fill(used, fill_ref[pl.program_id(1)], o_ref.shape[0], compute, o_ref)


def _expert_down(h, w_down, b_down, layer, plan):
    n_rows, f = h.shape
    _, n_exp, _, d = w_down.shape
    tm = EXPERT_TILE
    n_blocks = n_rows // tm
    return pl.pallas_call(
        functools.partial(_expert_down_kernel, layer=layer),
        grid_spec=pltpu.PrefetchScalarGridSpec(
            num_scalar_prefetch=5,
            grid=(1, n_blocks),
            in_specs=[pl.BlockSpec((tm, f), lambda n, j, be, od, nx, mt, fl: (j, 0)),
                      pl.BlockSpec(memory_space=pl.ANY),
                      pl.BlockSpec((1, 1, 1, d), lambda n, j, be, od, nx, mt, fl: (layer, be[j], 0, 0))],
            out_specs=pl.BlockSpec((tm, d // 2), lambda n, j, be, od, nx, mt, fl: (j, 0)),
            scratch_shapes=[pltpu.VMEM((2, f, d), F32), pltpu.VMEM((f, d), BF16), pltpu.SemaphoreType.DMA((2,))],
        ),
        out_shape=jax.ShapeDtypeStruct((n_rows, d // 2), U32),
        compiler_params=pltpu.CompilerParams(
            dimension_semantics=("arbitrary", "arbitrary"), vmem_limit_bytes=EXPERT_VMEM_LIMIT),
        name="expert_down",
    )(*plan, h, w_down, b_down.reshape(b_down.shape[0], n_exp, 1, d))


def _combine_kernel(destc_ref, destn_ref, outs_ref, w_ref, x_ref, mod_ref, gfin_ref, o_ref, buf, sem, *, final):
    t = pl.program_id(0)
    nt = pl.num_programs(0)
    tb = x_ref.shape[0]

    def issue(dest_ref, slot):
        def body(r, carry):
            for k in range(TOP_K):
                p = dest_ref[0, 0, r * TOP_K + k]
                pltpu.make_async_copy(outs_ref.at[pl.ds(p, 1)], buf.at[slot, k, pl.ds(r, 1)],
                                      sem.at[slot]).start(priority=k % DMA_QUEUES)
            return carry

        lax.fori_loop(0, tb, body, 0, unroll=4)

    @pl.when(t == 0)
    def _():
        issue(destc_ref, 0)

    @pl.when(t + 1 < nt)
    def _():
        issue(destn_ref, (t + 1) % 2)

    slot = t % 2

    def group(_, carry):
        for _ in range(WAIT_GROUP):
            pltpu.make_async_copy(outs_ref.at[pl.ds(0, 1)], buf.at[slot, 0, pl.ds(0, 1)], sem.at[slot]).wait()
        return carry

    lax.fori_loop(0, tb * TOP_K // WAIT_GROUP, group, 0)

    w = w_ref[...]
    half = buf.shape[3]
    ya = jnp.zeros((tb, half), F32)
    yb = jnp.zeros((tb, half), F32)
    for k in range(TOP_K):
        p = buf[slot, k]
        wk = w[:, k:k + 1]
        ya = ya + wk * lax.bitcast_convert_type(p & jnp.uint32(HI_MASK), F32)
        yb = yb + wk * lax.bitcast_convert_type(p << 16, F32)
    m = mod_ref[0, 0]
    xa = x_ref[:, 0:half] + m[5:6, 0:half] * ya
    xb = x_ref[:, half:2 * half] + m[5:6, half:2 * half] * yb
    if final:
        ms = (jnp.sum(xa * xa, axis=-1, keepdims=True) + jnp.sum(xb * xb, axis=-1, keepdims=True)) / (2 * half)
        scale = lax.rsqrt(ms + EPS)
        xa = xa * scale * gfin_ref[:, 0:half]
        xb = xb * scale * gfin_ref[:, half:2 * half]
    o_ref[:, 0:half] = xa
    o_ref[:, half:2 * half] = xb


def _combine(outs, dest, wt, x_flat, modt, g_final, tiles_per_batch, n_ctx_tiles, final):
    n_tok, d = x_flat.shape
    tb = COMBINE_TILE
    n_tiles = n_tok // tb
    cur = lambda t: (t, 0, 0)
    nxt = lambda t: (jnp.minimum(t + 1, n_tiles - 1), 0, 0)
    smem = lambda imap: pl.BlockSpec((1, 1, tb * TOP_K), imap, memory_space=pltpu.SMEM)
    dest3 = dest.reshape(n_tiles, 1, tb * TOP_K)
    return pl.pallas_call(
        functools.partial(_combine_kernel, final=final),
        grid=(n_tiles,),
        in_specs=[smem(cur), smem(nxt),
                  pl.BlockSpec(memory_space=pl.ANY),
                  pl.BlockSpec((tb, LANES), lambda t: (t, 0)),
                  pl.BlockSpec((tb, d), lambda t: (t, 0)),
                  pl.BlockSpec((1, 1, 6, d), lambda t: (
                      t // tiles_per_batch, jnp.where(t % tiles_per_batch < n_ctx_tiles, 0, 1), 0, 0)),
                  pl.BlockSpec((1, d), lambda t: (0, 0))],
        out_specs=pl.BlockSpec((tb, d), lambda t: (t, 0)),
        out_shape=jax.ShapeDtypeStruct((n_tok, d), F32),
        scratch_shapes=[pltpu.VMEM((2, TOP_K, tb, d // 2), U32), pltpu.SemaphoreType.DMA((2,))],
        compiler_params=pltpu.CompilerParams(
            dimension_semantics=("arbitrary",), vmem_limit_bytes=VMEM_LIMIT),
        name="expert_combine",
    )(dest3, dest3, outs, wt, x_flat, modt, g_final.reshape(1, d))


def _expert_layout(counts, n_assign):
    n_experts = counts.shape[0]
    tm = EXPERT_TILE
    padded = (counts + tm - 1) // tm * tm
    ends_p = jnp.cumsum(padded).astype(jnp.int32)
    starts_p = ends_p - padded
    n_blocks = (n_assign + n_experts * (tm - 1) + tm - 1) // tm
    block_start = jnp.arange(n_blocks, dtype=jnp.int32) * tm
    block_e = jnp.minimum(jnp.sum((ends_p[None, :] <= block_start[:, None]).astype(jnp.int32), axis=1),
                          n_experts - 1).astype(jnp.int32)
    n_used = ends_p[-1] // tm
    tile = jnp.arange(n_blocks, dtype=jnp.int32)
    used = tile < n_used
    first = jnp.logical_and(used, jnp.concatenate([jnp.ones((1,), bool), block_e[1:] != block_e[:-1]]))
    ordinal = (jnp.cumsum(first.astype(jnp.int32)) - 1).astype(jnp.int32)
    first_pos = jnp.where(first, tile, n_blocks)
    after = jnp.concatenate([first_pos[1:], jnp.full((1,), n_blocks, jnp.int32)])
    next_first = lax.cummin(after, axis=0, reverse=True)
    following = jnp.where(next_first < n_blocks, block_e[jnp.minimum(next_first, n_blocks - 1)], -1).astype(jnp.int32)
    meta = jnp.stack([n_used, jnp.sum(first.astype(jnp.int32))]).astype(jnp.int32)
    filled = jnp.clip((starts_p + counts)[block_e] - tile * tm, 0, tm).astype(jnp.int32)
    plan = (block_e, ordinal, following, meta, filled)
    return starts_p, starts_p + counts, padded - counts, plan, n_blocks * tm


def _moe(tokp, route, counts, w_gu, b_gu, w_down, b_down, layer):
    n_tok = tokp.shape[0]
    n_experts = counts.shape[0]
    idx4 = route[:, :TOP_K]
    rank4 = route[:, TOP_K:2 * TOP_K]
    starts_p, zstart, zlen, plan, n_rows = _expert_layout(counts, n_tok * TOP_K)
    onehot = idx4[:, :, None] == jnp.arange(n_experts, dtype=jnp.int32)[None, None, :]
    dest = (jnp.sum(jnp.where(onehot, starts_p[None, None, :], 0), axis=-1) + rank4).astype(jnp.int32)
    xs = _dispatch(tokp, dest, zstart, zlen, n_rows)
    h = _expert_up(xs, w_gu, b_gu, layer, plan)
    outs = _expert_down(h, w_down, b_down, layer, plan)
    return outs, dest


def kernel(x, c, ctx, c_ctx, w_mod, b_mod, w_in, conv_w, conv_b, w_r, b_r, w_i, b_i, lam, g_v, w_s, b_s,
           g_lru, g_mlp, w_out, w_router, b_router, w_gu, b_gu, w_down, b_down, g_final):
    bsz, n_lat, d = x.shape
    n_ctx = ctx.shape[1]
    depth = w_mod.shape[0]
    n_experts = w_router.shape[2]
    r = conv_w.shape[2]
    tt = TIME_TILE
    assert n_ctx % tt == 0 and n_lat % tt == 0 and tt % CHUNK == 0
    assert tt % DISPATCH_TILE == 0 and tt % COMBINE_TILE == 0
    assert n_experts <= LANES and r % LRU_HEADS == 0
    n_ctx_tiles = n_ctx // tt

    rows = (bsz + 1 + SUBLANES - 1) // SUBLANES * SUBLANES
    c_all = jnp.zeros((rows, d), F32).at[:bsz].set(c).at[bsz].set(c_ctx)
    mod = _modulation(c_all, w_mod, b_mod)

    hcat = jnp.concatenate([ctx, x], axis=1)
    out = None
    for l in range(depth):
        last = l == depth - 1
        mod_x = mod[l, :bsz].reshape(bsz, 1, 6, d)
        mod_c = jnp.broadcast_to(mod[l, bsz].reshape(1, 1, 6, d), (bsz, 1, 6, d))
        modt = jnp.concatenate([mod_c, mod_x], axis=1)

        z = _inproj(hcat, modt, w_in[l].astype(BF16), n_ctx_tiles)
        hd = r // LRU_HEADS
        wg = jnp.concatenate([w_r[l], w_i[l]], axis=-1).astype(BF16)
        bg = jnp.concatenate([b_r[l], b_i[l]], axis=-1).reshape(2, LRU_HEADS, 1, 2 * hd)
        hf, hb = _scan(z, conv_w[l], conv_b[l].reshape(1, r), wg, bg, lam[l], n_ctx_tiles)

        wr_hi = w_router[l].astype(BF16)
        wr_lo = (w_router[l] - wr_hi.astype(F32)).astype(BF16)
        wr_split = jnp.pad(jnp.stack([wr_hi, wr_lo]), ((0, 0), (0, 0), (0, LANES - n_experts)))
        br_pad = jnp.pad(b_router[l], (0, LANES - n_experts)).reshape(1, LANES)
        off = n_ctx_tiles if last else 0
        xo, tokp, route, wt, cnt = _finish(
            z, hf, hb, hcat, modt, g_lru[l].reshape(1, r), g_mlp[l].reshape(1, r), g_v[l].reshape(1, r),
            w_s[l].astype(BF16), b_s[l].reshape(MLP_HEADS, CHUNK, 1), w_out[l].astype(BF16),
            wr_split, br_pad, n_experts, off, n_ctx_tiles)

        rows_out = xo.shape[1]
        n_tok = bsz * rows_out
        outs, dest = _moe(
            tokp.reshape(n_tok, d // 2), route.reshape(n_tok, LANES), cnt[0, :n_experts],
            w_gu, b_gu, w_down, b_down, l)
        new = _combine(outs, dest, wt.reshape(n_tok, LANES), xo.reshape(n_tok, d), modt,
                       g_final, rows_out // COMBINE_TILE, 0 if last else n_ctx // COMBINE_TILE, last)
        new = new.reshape(bsz, rows_out, d)
        if last:
            out = new
        else:
            hcat = new
    return out
```
